```python
import math
import jax, jax.numpy as jnp
from jax import lax
import numpy as np

D_MODEL = 1024
BATCH = 16
SEQ = 4096
DEPTH = 4

ATTN_HEADS = 8
ATTN_HEAD_DIM = 64
ATTN_WIDTH = ATTN_HEADS * ATTN_HEAD_DIM
Q_BLOCK = 128
SSM_GROUPS = 32
SSM_GROUP_CH = 16
SSM_WIDTH = SSM_GROUPS * SSM_GROUP_CH
SSM_STATE = 64
D_FF = 4 * D_MODEL
N_IN = 3 * ATTN_WIDTH + ATTN_HEADS + SSM_WIDTH + 2 * D_MODEL
RMS_EPS = 1e-6
DT_MIN = 1e-3
DT_MAX = 1e-1

kernel_name = 'fox_s5_gated_hybrid_trunk'


def rmsnorm(x, g):
    xf = x.astype(jnp.float32)
    xf = xf * lax.rsqrt(jnp.mean(xf * xf, axis=-1, keepdims=True) + RMS_EPS)
    return (xf * g.astype(jnp.float32)).astype(x.dtype)


def forgetting_attention(q, k, v, log_f):
    seq = q.shape[2]
    scale = ATTN_HEAD_DIM ** -0.5
    cum = jnp.cumsum(log_f, axis=-1)
    outs = []
    for i in range(seq // Q_BLOCK):
        lo, hi = i * Q_BLOCK, (i + 1) * Q_BLOCK
        s = jnp.einsum('bhqd,bhkd->bhqk', q[:, :, lo:hi], k[:, :, :hi]).astype(jnp.float32) * scale
        s = s + cum[:, :, lo:hi, None] - cum[:, :, None, :hi]
        causal = (lo + jnp.arange(Q_BLOCK))[:, None] >= jnp.arange(hi)[None, :]
        p = jax.nn.softmax(jnp.where(causal, s, -jnp.inf), axis=-1)
        outs.append(jnp.einsum('bhqk,bhkd->bhqd', p.astype(v.dtype), v[:, :, :hi]))
    return jnp.concatenate(outs, axis=2)


def _linear_recurrence(e1, e2):
    a1, b1 = e1
    a2, b2 = e2
    return a1 * a2, a2 * b1 + b2


def s5_ssm(u, lam_re, lam_im, log_dt, b_re, b_im, c_re, c_im, d_skip):
    bsz, seq, _ = u.shape
    f32 = jnp.float32
    ug = u.astype(f32).reshape(bsz, seq, SSM_GROUPS, SSM_GROUP_CH)
    lam = lax.complex(lam_re.astype(f32), lam_im.astype(f32))
    dt = jnp.exp(log_dt.astype(f32))[:, None]
    lam_bar = jnp.exp(lam * dt)
    b_mat = lax.complex(b_re.astype(f32), b_im.astype(f32))
    b_bar = ((lam_bar - 1.0) / lam)[:, :, None] * b_mat
    bu = jnp.einsum('bsgc,gpc->bsgp', ug.astype(jnp.complex64), b_bar)
    a = jnp.broadcast_to(lam_bar[None, None], (1, seq, SSM_GROUPS, SSM_STATE))
    _, states = lax.associative_scan(_linear_recurrence, (a, bu), axis=1)
    c_mat = lax.complex(c_re.astype(f32), c_im.astype(f32))
    y = jnp.einsum('bsgp,gcp->bsgc', states, c_mat).real
    y = y + d_skip.astype(f32).reshape(SSM_GROUPS, SSM_GROUP_CH) * ug
    return y.reshape(bsz, seq, SSM_WIDTH).astype(u.dtype)


def hybrid_layer(x, norm_mix, w_in, b_forget, lam_re, lam_im, log_dt, b_re, b_im,
                 c_re, c_im, d_skip, w_glu, b_glu, w_branch_a, w_branch_b, w_out,
                 norm_mlp, w_mlp_up, w_mlp_down):
    bsz, seq, _ = x.shape
    h = rmsnorm(x, norm_mix)
    proj = h @ w_in
    o1 = ATTN_WIDTH
    o2 = o1 + ATTN_WIDTH
    o3 = o2 + ATTN_WIDTH
    o4 = o3 + ATTN_HEADS
    o5 = o4 + SSM_WIDTH
    o6 = o5 + D_MODEL
    q, k, v, f_logit, u, gate_a, gate_b = jnp.split(proj, [o1, o2, o3, o4, o5, o6], axis=-1)

    def heads(t):
        return t.reshape(bsz, seq, ATTN_HEADS, ATTN_HEAD_DIM).transpose(0, 2, 1, 3)
    log_f = jax.nn.log_sigmoid((f_logit + b_forget).astype(jnp.float32)).transpose(0, 2, 1)
    y_a = forgetting_attention(heads(q), heads(k), heads(v), log_f)
    y_a = y_a.transpose(0, 2, 1, 3).reshape(bsz, seq, ATTN_WIDTH)

    y_b = jax.nn.gelu(s5_ssm(u, lam_re, lam_im, log_dt, b_re, b_im, c_re, c_im, d_skip))
    y_b = y_b * jax.nn.sigmoid(y_b @ w_glu + b_glu)

    mixed = jax.nn.sigmoid(gate_a) * (y_a @ w_branch_a) + jax.nn.sigmoid(gate_b) * (y_b @ w_branch_b)
    x = x + mixed @ w_out

    h = rmsnorm(x, norm_mlp)
    x = x + jnp.square(jax.nn.relu(h @ w_mlp_up)) @ w_mlp_down
    return x


def setup_inputs(seed: int = 0) -> dict:
    key = jax.random.key(seed)
    ks = jax.random.split(key, 24)
    f32 = jnp.float32
    L, G, P, C = DEPTH, SSM_GROUPS, SSM_STATE, SSM_GROUP_CH

    def nrm(k, shape, scale):
        return jax.random.normal(k, shape, f32) * scale

    n_idx = jnp.arange(P, dtype=f32)
    return {
        'x': nrm(ks[0], (BATCH, SEQ, D_MODEL), 1.0),
        'norm_mix': 1.0 + nrm(ks[1], (L, D_MODEL), 0.02),
        'w_in': nrm(ks[2], (L, D_MODEL, N_IN), D_MODEL ** -0.5),
        'b_forget': jax.random.uniform(ks[3], (L, ATTN_HEADS), f32, 1.0, 5.0),
        'ssm_lambda_re': -0.5 + nrm(ks[4], (L, G, P), 0.01),
        'ssm_lambda_im': jnp.pi * n_idx + nrm(ks[5], (L, G, P), 0.01),
        'ssm_log_dt': jax.random.uniform(ks[6], (L, G), f32, math.log(DT_MIN), math.log(DT_MAX)),
        'ssm_b_re': nrm(ks[7], (L, G, P, C), (2 * C) ** -0.5),
        'ssm_b_im': nrm(ks[8], (L, G, P, C), (2 * C) ** -0.5),
        'ssm_c_re': nrm(ks[9], (L, G, C, P), P ** -0.5),
        'ssm_c_im': nrm(ks[10], (L, G, C, P), P ** -0.5),
        'ssm_d': nrm(ks[11], (L, SSM_WIDTH), 1.0),
        'w_glu': nrm(ks[12], (L, SSM_WIDTH, SSM_WIDTH), SSM_WIDTH ** -0.5),
        'b_glu': nrm(ks[13], (L, SSM_WIDTH), 0.01),
        'w_branch_a': nrm(ks[14], (L, ATTN_WIDTH, D_MODEL), ATTN_WIDTH ** -0.5),
        'w_branch_b': nrm(ks[15], (L, SSM_WIDTH, D_MODEL), SSM_WIDTH ** -0.5),
        'w_out': nrm(ks[16], (L, D_MODEL, D_MODEL), D_MODEL ** -0.5),
        'norm_mlp': 1.0 + nrm(ks[17], (L, D_MODEL), 0.02),
        'w_mlp_up': nrm(ks[18], (L, D_MODEL, D_FF), D_MODEL ** -0.5),
        'w_mlp_down': nrm(ks[19], (L, D_FF, D_MODEL), D_FF ** -0.5),
        'norm_final': 1.0 + nrm(ks[20], (D_MODEL,), 0.02),
    }


def reference(x, norm_mix, w_in, b_forget, ssm_lambda_re, ssm_lambda_im, ssm_log_dt,
              ssm_b_re, ssm_b_im, ssm_c_re, ssm_c_im, ssm_d, w_glu, b_glu,
              w_branch_a, w_branch_b, w_out, norm_mlp, w_mlp_up, w_mlp_down, norm_final):
    for l in range(DEPTH):
        x = hybrid_layer(x, norm_mix[l], w_in[l], b_forget[l], ssm_lambda_re[l], ssm_lambda_im[l],
                         ssm_log_dt[l], ssm_b_re[l], ssm_b_im[l], ssm_c_re[l], ssm_c_im[l],
                         ssm_d[l], w_glu[l], b_glu[l], w_branch_a[l], w_branch_b[l], w_out[l],
                         norm_mlp[l], w_mlp_up[l], w_mlp_down[l])
    return rmsnorm(x, norm_final)
```

```python
import functools

import jax
import jax.numpy as jnp
from jax import lax
from jax.experimental import pallas as pl
from jax.experimental.pallas import tpu as pltpu

F32 = jnp.float32
BF16 = jnp.bfloat16

D_MODEL = 1024
NUM_HEADS = 8
HEAD_DIM = 64
ATTN_WIDTH = NUM_HEADS * HEAD_DIM
SSM_GROUPS = 32
SSM_GROUP_CH = 16
SSM_STATE = 64
SSM_WIDTH = SSM_GROUPS * SSM_GROUP_CH
N_STATES = SSM_GROUPS * SSM_STATE
D_FF = 4 * D_MODEL
RMS_EPS = 1e-6
MASK_VALUE = -1e30

LANES = 128
SUBLANES = 8
VMEM_LIMIT_BYTES = 56 * 1024 * 1024

TM_PROJ = 512
TQ = 256
TK = 256
LT = 64
SSM_BATCH = SUBLANES
SSM_PITCH = LT + SUBLANES
SCAN_SLABS = 8
TM_MLP = 512
FF_CHUNK = 1024

_dot = functools.partial(jnp.dot, preferred_element_type=F32)


def _rmsnorm(x, g):
    ms = jnp.mean(x * x, axis=-1, keepdims=True)
    return x * lax.rsqrt(ms + RMS_EPS) * g


def _const_spec(shape):
    zeros = (0,) * len(shape)
    return pl.BlockSpec(shape, lambda *_: zeros, pipeline_mode=pl.Buffered(1))


def _cumsum_lanes(x):
    n = x.shape[-1]
    lane = lax.broadcasted_iota(jnp.int32, x.shape, x.ndim - 1)
    k = 1
    while k < n:
        x = x + jnp.where(lane >= k, pltpu.roll(x, k, axis=x.ndim - 1), 0.0)
        k *= 2
    return x


def _inproj_kernel(x_ref, g_ref, wq_ref, wk_ref, wv_ref, wf_ref, bf_ref, wu_ref, wga_ref, wgb_ref,
                   q_ref, k_ref, v_ref, cum_ref, u_ref, ga_ref, gb_ref, carry_ref):
    @pl.when(pl.program_id(1) == 0)
    def _():
        carry_ref[...] = jnp.zeros_like(carry_ref)

    tm = x_ref.shape[1]
    h = _rmsnorm(x_ref[0], g_ref[...]).astype(BF16)
    q_ref[0] = _dot(h, wq_ref[...]).astype(BF16)
    k_ref[0] = _dot(h, wk_ref[...]).astype(BF16)
    v_ref[0] = _dot(h, wv_ref[...]).astype(BF16)
    u_ref[0] = _dot(h, wu_ref[...]).astype(BF16)
    ga_ref[0] = jax.nn.sigmoid(_dot(h, wga_ref[...])).astype(BF16)
    gb_ref[0] = jax.nn.sigmoid(_dot(h, wgb_ref[...])).astype(BF16)

    fl = _dot(h, wf_ref[...]) + bf_ref[...]
    log_f = jnp.minimum(fl, 0.0) - jnp.log1p(jnp.exp(-jnp.abs(fl)))
    cum = _cumsum_lanes(log_f.T[:NUM_HEADS]) + carry_ref[:, 0:1]
    for j in range(tm // TK):
        cum_ref[0, j] = cum[:, j * TK:(j + 1) * TK]
    carry_ref[...] = jnp.broadcast_to(cum[:, tm - 1:tm], carry_ref.shape)


def _inproj(x, g, wq, wk, wv, wf, bf, wu, wga, wgb):
    b, s, _ = x.shape
    tm = min(TM_PROJ, s)
    tok = lambda width: pl.BlockSpec((1, tm, width), lambda i, j: (i, j, 0))
    act = lambda width: jax.ShapeDtypeStruct((b, s, width), BF16)
    return pl.pallas_call(
        _inproj_kernel,
        grid=(b, s // tm),
        in_specs=[tok(D_MODEL), _const_spec(g.shape), _const_spec(wq.shape), _const_spec(wk.shape),
                  _const_spec(wv.shape), _const_spec(wf.shape), _const_spec(bf.shape),
                  _const_spec(wu.shape), _const_spec(wga.shape), _const_spec(wgb.shape)],
        out_specs=[tok(ATTN_WIDTH), tok(ATTN_WIDTH), tok(ATTN_WIDTH),
                   pl.BlockSpec((1, tm // TK, NUM_HEADS, TK), lambda i, j: (i, j, 0, 0)),
                   tok(SSM_WIDTH), tok(D_MODEL), tok(D_MODEL)],
        out_shape=[act(ATTN_WIDTH), act(ATTN_WIDTH), act(ATTN_WIDTH),
                   jax.ShapeDtypeStruct((b, s // TK, NUM_HEADS, TK), F32),
                   act(SSM_WIDTH), act(D_MODEL), act(D_MODEL)],
        scratch_shapes=[pltpu.VMEM((NUM_HEADS, LANES), F32)],
        compiler_params=pltpu.CompilerParams(
            dimension_semantics=("arbitrary", "arbitrary"), vmem_limit_bytes=VMEM_LIMIT_BYTES),
        name="inproj",
    )(x, g, wq, wk, wv, wf, bf, wu, wga, wgb)


def _attn_kernel(q_ref, k_ref, v_ref, cum_ref, o_ref):
    i = pl.program_id(1)
    tq = q_ref.shape[1]
    low = lax.broadcasted_iota(jnp.int32, (tq, LANES), 1) < HEAD_DIM
    causal = (lax.broadcasted_iota(jnp.int32, (tq, TK), 0)
              >= lax.broadcasted_iota(jnp.int32, (tq, TK), 1))

    for pair in range(NUM_HEADS // 2):
        ls = slice(LANES * pair, LANES * (pair + 1))
        q_pair = q_ref[0, :, ls]
        outs = []
        for hh in range(2):
            head = 2 * pair + hh
            qh = jnp.where(low if hh == 0 else jnp.logical_not(low), q_pair, jnp.zeros_like(q_pair))

            def scores(j, qh=qh, head=head, ls=ls):
                kb = k_ref[0, pl.ds(pl.multiple_of(j * TK, TK), TK), ls]
                s = lax.dot_general(qh, kb, (((1,), (1,)), ((), ())), preferred_element_type=F32)
                return s - cum_ref[0, j, head:head + 1, :]

            def update(z, j, m, l, acc, ls=ls):
                m_new = jnp.maximum(m, jnp.max(z, axis=-1, keepdims=True))
                alpha = jnp.exp(m - m_new)
                p = jnp.exp(z - m_new)
                l_new = alpha * l + jnp.sum(p, axis=-1, keepdims=True)
                vb = v_ref[0, pl.ds(pl.multiple_of(j * TK, TK), TK), ls]
                return m_new, l_new, alpha * acc + _dot(p.astype(BF16), vb)

            def body(j, carry, scores=scores, update=update):
                return update(scores(j), j, *carry)

            init = (jnp.full((tq, 1), MASK_VALUE, F32), jnp.zeros((tq, 1), F32),
                    jnp.zeros((tq, LANES), F32))
            carry = lax.fori_loop(0, i, body, init)
            _, l, acc = update(jnp.where(causal, scores(i), MASK_VALUE), i, *carry)
            outs.append(acc / l)
        o_ref[0, :, ls] = jnp.where(low, outs[0], outs[1]).astype(BF16)


def _attention(q, k, v, cum):
    b, s, _ = q.shape
    assert TQ == TK and s % TQ == 0
    return pl.pallas_call(
        _attn_kernel,
        grid=(b, s // TQ),
        in_specs=[pl.BlockSpec((1, TQ, ATTN_WIDTH), lambda i, j: (i, j, 0)),
                  pl.BlockSpec((1, s, ATTN_WIDTH), lambda i, j: (i, 0, 0)),
                  pl.BlockSpec((1, s, ATTN_WIDTH), lambda i, j: (i, 0, 0)),
                  pl.BlockSpec((1, s // TK, NUM_HEADS, TK), lambda i, j: (i, 0, 0, 0))],
        out_specs=pl.BlockSpec((1, TQ, ATTN_WIDTH), lambda i, j: (i, j, 0)),
        out_shape=jax.ShapeDtypeStruct((b, s, ATTN_WIDTH), BF16),
        compiler_params=pltpu.CompilerParams(
            dimension_semantics=("arbitrary", "arbitrary"), vmem_limit_bytes=VMEM_LIMIT_BYTES),
        name="attention",
    )(q, k, v, cum)


def _ssm_kernel(u_ref, bw_ref, ar_ref, ai_ref, cw_ref, d_ref, wglu_ref, bglu_ref, o_ref,
                st_ref, xr_ref, xi_ref):
    @pl.when(pl.program_id(1) == 0)
    def _():
        xr_ref[...] = jnp.zeros_like(xr_ref)
        xi_ref[...] = jnp.zeros_like(xi_ref)

    nb, lt, width = u_ref.shape
    rows = nb * lt
    half_w = width // 2
    half_s = N_STATES // 2
    slabs_half = half_s // LANES
    slabs_part = N_STATES // LANES
    u = u_ref[...].reshape(rows, width)

    for part in range(4):
        half = part % 2
        bu = _dot(u[:, half_w * half:half_w * (half + 1)], bw_ref[part])
        for s in range(slabs_half):
            slab = (part // 2) * slabs_part + half * slabs_half + s
            for b in range(nb):
                st_ref[slab, b * SSM_PITCH:b * SSM_PITCH + lt, :] = (
                    bu[b * lt:(b + 1) * lt, LANES * s:LANES * (s + 1)])

    for chunk in range(slabs_part // SCAN_SLABS):
        base = chunk * SCAN_SLABS
        cols = [slice(LANES * (base + s), LANES * (base + s + 1)) for s in range(SCAN_SLABS)]
        ar = [jnp.broadcast_to(ar_ref[:, c], (nb, LANES)) for c in cols]
        ai = [jnp.broadcast_to(ai_ref[:, c], (nb, LANES)) for c in cols]

        def step(t, carry, base=base, ar=ar, ai=ai):
            xr, xi = carry
            nr, ni = [], []
            for s in range(SCAN_SLABS):
                idx = pl.ds(t, nb, stride=SSM_PITCH)
                br = st_ref[base + s, idx, :]
                bi = st_ref[slabs_part + base + s, idx, :]
                r = ar[s] * xr[s] - ai[s] * xi[s] + br
                m = ar[s] * xi[s] + ai[s] * xr[s] + bi
                st_ref[base + s, idx, :] = r
                st_ref[slabs_part + base + s, idx, :] = m
                nr.append(r)
                ni.append(m)
            return tuple(nr), tuple(ni)

        init = (tuple(xr_ref[:, c] for c in cols), tuple(xi_ref[:, c] for c in cols))
        xr, xi = lax.fori_loop(0, lt, step, init)
        for s, c in enumerate(cols):
            xr_ref[:, c] = xr[s]
            xi_ref[:, c] = xi[s]

    def states(slab):
        return jnp.concatenate(
            [st_ref[slab, b * SSM_PITCH:b * SSM_PITCH + lt, :] for b in range(nb)], axis=0).astype(BF16)

    ys = []
    for half in range(2):
        x_re = jnp.concatenate([states(half * slabs_half + s) for s in range(slabs_half)], axis=1)
        x_im = jnp.concatenate(
            [states(slabs_part + half * slabs_half + s) for s in range(slabs_half)], axis=1)
        ys.append(_dot(x_re, cw_ref[half]) + _dot(x_im, cw_ref[2 + half]))
    y = jnp.concatenate(ys, axis=1) + d_ref[...] * u.astype(F32)

    y = jax.nn.gelu(y, approximate=True)
    y = y * jax.nn.sigmoid(_dot(y.astype(BF16), wglu_ref[...]) + bglu_ref[...])
    o_ref[...] = y.reshape(nb, lt, width).astype(BF16)


def _ssm(u, bw, ar, ai, cw, d, wglu, bglu):
    b, s, _ = u.shape
    assert b % SSM_BATCH == 0 and s % LT == 0
    tok = pl.BlockSpec((SSM_BATCH, LT, SSM_WIDTH), lambda i, j: (i, j, 0))
    return pl.pallas_call(
        _ssm_kernel,
        grid=(b // SSM_BATCH, s // LT),
        in_specs=[tok, _const_spec(bw.shape), _const_spec(ar.shape), _const_spec(ai.shape),
                  _const_spec(cw.shape), _const_spec(d.shape), _const_spec(wglu.shape),
                  _const_spec(bglu.shape)],
        out_specs=tok,
        out_shape=jax.ShapeDtypeStruct(u.shape, BF16),
        scratch_shapes=[pltpu.VMEM((2 * N_STATES // LANES, SSM_BATCH * SSM_PITCH, LANES), F32),
                        pltpu.VMEM((SSM_BATCH, N_STATES), F32),
                        pltpu.VMEM((SSM_BATCH, N_STATES), F32)],
        compiler_params=pltpu.CompilerParams(
            dimension_semantics=("arbitrary", "arbitrary"), vmem_limit_bytes=VMEM_LIMIT_BYTES),
        name="s5",
    )(u, bw, ar, ai, cw, d, wglu, bglu)


def _merge_mlp_kernel(x_ref, ya_ref, yb_ref, ga_ref, gb_ref, wa_ref, wb_ref, wo_ref, g_ref,
                      wup_ref, wdn_ref, gfin_ref, o_ref, *, final_norm):
    mixed = (ga_ref[...].astype(F32) * _dot(ya_ref[...], wa_ref[...])
             + gb_ref[...].astype(F32) * _dot(yb_ref[...], wb_ref[...]))
    x = x_ref[...] + _dot(mixed.astype(BF16), wo_ref[...])
    h = _rmsnorm(x, g_ref[...]).astype(BF16)
    for c in range(D_FF // FF_CHUNK):
        cs = slice(c * FF_CHUNK, (c + 1) * FF_CHUNK)
        up = jnp.maximum(_dot(h, wup_ref[:, cs]), 0.0)
        x = x + _dot((up * up).astype(BF16), wdn_ref[cs, :])
    if final_norm:
        x = _rmsnorm(x, gfin_ref[...])
    o_ref[...] = x


def _merge_mlp(x, ya, yb, ga, gb, wa, wb, wo, g, wup, wdn, gfin, final_norm):
    t = x.shape[0]
    tm = min(TM_MLP, t)
    tok = lambda width: pl.BlockSpec((tm, width), lambda i: (i, 0))
    return pl.pallas_call(
        functools.partial(_merge_mlp_kernel, final_norm=final_norm),
        grid=(t // tm,),
        in_specs=[tok(D_MODEL), tok(ATTN_WIDTH), tok(SSM_WIDTH), tok(D_MODEL), tok(D_MODEL),
                  _const_spec(wa.shape), _const_spec(wb.shape), _const_spec(wo.shape),
                  _const_spec(g.shape), _const_spec(wup.shape), _const_spec(wdn.shape),
                  _const_spec(gfin.shape)],
        out_specs=tok(D_MODEL),
        out_shape=jax.ShapeDtypeStruct(x.shape, F32),
        compiler_params=pltpu.CompilerParams(
            dimension_semantics=("arbitrary",), vmem_limit_bytes=VMEM_LIMIT_BYTES),
        name="merge_mlp",
    )(x, ya, yb, ga, gb, wa, wb, wo, g, wup, wdn, gfin)


def _ssm_params(lam_re, lam_im, log_dt, b_re, b_im, c_re, c_im):
    dt = jnp.exp(log_dt)[:, None]
    mag = jnp.exp(lam_re * dt)
    ar = mag * jnp.cos(lam_im * dt)
    ai = mag * jnp.sin(lam_im * dt)
    den = lam_re * lam_re + lam_im * lam_im
    zr = ((ar - 1.0) * lam_re + ai * lam_im) / den
    zi = (ai * lam_re - (ar - 1.0) * lam_im) / den
    bb_re = zr[:, :, None] * b_re - zi[:, :, None] * b_im
    bb_im = zr[:, :, None] * b_im + zi[:, :, None] * b_re

    hg = SSM_GROUPS // 2
    eye = jnp.eye(hg, dtype=F32)

    def b_block(m):
        return jnp.einsum("gpc,gh->gchp", m, eye).reshape(hg * SSM_GROUP_CH, hg * SSM_STATE)

    def c_block(m):
        return jnp.einsum("gcp,gh->gphc", m, eye).reshape(hg * SSM_STATE, hg * SSM_GROUP_CH)

    bw = jnp.stack([b_block(bb_re[:hg]), b_block(bb_re[hg:]), b_block(bb_im[:hg]), b_block(bb_im[hg:])])
    cw = jnp.stack([c_block(c_re[:hg]), c_block(c_re[hg:]), c_block(-c_im[:hg]), c_block(-c_im[hg:])])
    return bw.astype(BF16), ar.reshape(1, N_STATES), ai.reshape(1, N_STATES), cw.astype(BF16)


def _split_w_in(w_in, b_forget):
    o1 = ATTN_WIDTH
    o2 = o1 + ATTN_WIDTH
    o3 = o2 + ATTN_WIDTH
    o4 = o3 + NUM_HEADS
    o5 = o4 + SSM_WIDTH
    o6 = o5 + D_MODEL
    wq = (w_in[:, :o1] * HEAD_DIM ** -0.5).astype(BF16)
    wf = jnp.zeros((D_MODEL, LANES), F32).at[:, :NUM_HEADS].set(w_in[:, o3:o4]).astype(BF16)
    bf = jnp.zeros((1, LANES), F32).at[0, :NUM_HEADS].set(b_forget)
    return (wq, w_in[:, o1:o2].astype(BF16), w_in[:, o2:o3].astype(BF16), wf, bf,
            w_in[:, o4:o5].astype(BF16), w_in[:, o5:o6].astype(BF16), w_in[:, o6:].astype(BF16))


@jax.jit
def kernel(x, norm_mix, w_in, b_forget, ssm_lambda_re, ssm_lambda_im, ssm_log_dt, ssm_b_re, ssm_b_im,
           ssm_c_re, ssm_c_im, ssm_d, w_glu, b_glu, w_branch_a, w_branch_b, w_out, norm_mlp,
           w_mlp_up, w_mlp_down, norm_final):
    b, s, d = x.shape
    depth = w_in.shape[0]
    gfin = norm_final.reshape(1, d)
    for l in range(depth):
        wq, wk, wv, wf, bf, wu, wga, wgb = _split_w_in(w_in[l], b_forget[l])
        q, k, v, cum, u, ga, gb = _inproj(x, norm_mix[l].reshape(1, d), wq, wk, wv, wf, bf, wu, wga, wgb)
        ya = _attention(q, k, v, cum)
        bw, ar, ai, cw = _ssm_params(ssm_lambda_re[l], ssm_lambda_im[l], ssm_log_dt[l],
                                     ssm_b_re[l], ssm_b_im[l], ssm_c_re[l], ssm_c_im[l])
        yb = _ssm(u, bw, ar, ai, cw, ssm_d[l].reshape(1, SSM_WIDTH), w_glu[l].astype(BF16),
                  b_glu[l].reshape(1, SSM_WIDTH))
        t = b * s
        x = _merge_mlp(x.reshape(t, d), ya.reshape(t, ATTN_WIDTH), yb.reshape(t, SSM_WIDTH),
                       ga.reshape(t, d), gb.reshape(t, d), w_branch_a[l].astype(BF16),
                       w_branch_b[l].astype(BF16), w_out[l].astype(BF16), norm_mlp[l].reshape(1, d),
                       w_mlp_up[l].astype(BF16), w_mlp_down[l].astype(BF16), gfin,
                       final_norm=(l == depth - 1)).reshape(b, s, d)
    return x
```

```python
import functools
import math

import jax
import jax.numpy as jnp
from jax import lax
from jax.experimental import pallas as pl
from jax.experimental.pallas import tpu as pltpu

F32 = jnp.float32
BF16 = jnp.bfloat16

D_MODEL = 1024
NUM_HEADS = 8
HEAD_DIM = 64
ATTN_WIDTH = NUM_HEADS * HEAD_DIM
SSM_GROUPS = 32
SSM_GROUP_CH = 16
SSM_STATE = 64
SSM_WIDTH = SSM_GROUPS * SSM_GROUP_CH
N_STATES = SSM_GROUPS * SSM_STATE
D_FF = 4 * D_MODEL
RMS_EPS = 1e-6
MASK_VALUE = -1e30
LOG2E = math.log2(math.e)
BIAS_PIECES = 3

LANES = 128
SUBLANES = 8
VMEM_LIMIT_BYTES = 56 * 1024 * 1024

TM_PROJ = 512
TQ = 512
TK = 256
LT = 64
SSM_BATCH = SUBLANES
SSM_PITCH = LT + SUBLANES
SCAN_SLABS = 8
TM_MLP = 512
FF_CHUNK = 1024

_dot = functools.partial(jnp.dot, preferred_element_type=F32)


def _rmsnorm(x, g):
    ms = jnp.mean(x * x, axis=-1, keepdims=True)
    return x * lax.rsqrt(ms + RMS_EPS) * g


def _const_spec(shape):
    zeros = (0,) * len(shape)
    return pl.BlockSpec(shape, lambda *_: zeros, pipeline_mode=pl.Buffered(1))


def _low_half(shape):
    lane = lax.broadcasted_iota(jnp.int32, shape, len(shape) - 1)
    return (lane & (LANES - 1)) < HEAD_DIM


def _cumsum_rows(x):
    n = x.shape[0]
    row = lax.broadcasted_iota(jnp.int32, x.shape, 0)
    k = 1
    while k < n:
        x = x + jnp.where(row >= k, pltpu.roll(x, k, axis=0), 0.0)
        k *= 2
    return x


def _inproj_kernel(x_ref, g_ref, wq_ref, wk_ref, wv_ref, wf_ref, bf_ref, pe_ref, po_ref, one_ref,
                   wu_ref, wga_ref, wgb_ref,
                   qe_ref, qo_ref, ke_ref, ko_ref, ve_ref, vo_ref, u_ref, ga_ref, gb_ref, carry_ref):
    @pl.when(pl.program_id(1) == 0)
    def _():
        carry_ref[...] = jnp.zeros_like(carry_ref)

    tm = x_ref.shape[1]
    h = _rmsnorm(x_ref[0], g_ref[...]).astype(BF16)
    low = _low_half((tm, ATTN_WIDTH))
    q_ones = one_ref[0:1, :]
    q_ones_odd = one_ref[1:2, :]

    q = _dot(h, wq_ref[...])
    qe_ref[0] = jnp.where(low, q, q_ones).astype(BF16)
    qo_ref[0] = jnp.where(low, q_ones_odd, q).astype(BF16)
    v = _dot(h, wv_ref[...])
    ve_ref[0] = jnp.where(low, v, 1.0).astype(BF16)
    vo_ref[0] = jnp.where(low, 1.0, v).astype(BF16)

    fl = _dot(h, wf_ref[...]) + bf_ref[...]
    log_f = jnp.minimum(fl, 0.0) - jnp.log1p(jnp.exp(-jnp.abs(fl)))
    cum = _cumsum_rows(log_f) + carry_ref[0:1, :]
    carry_ref[...] = jnp.broadcast_to(cum[tm - 1:tm, :], carry_ref.shape)
    bias = cum * (-LOG2E)
    hi = bias.astype(BF16).astype(F32)
    mid = (bias - hi).astype(BF16).astype(F32)
    lo = (bias - hi - mid).astype(BF16).astype(F32)
    lane = lax.broadcasted_iota(jnp.int32, bias.shape, 1)
    pieces = jnp.where(lane < NUM_HEADS, hi, jnp.where(lane < 2 * NUM_HEADS, mid, lo)).astype(BF16)
    k = _dot(h, wk_ref[...])
    ke_ref[0] = jnp.where(low, k, _dot(pieces, pe_ref[...])).astype(BF16)
    ko_ref[0] = jnp.where(low, _dot(pieces, po_ref[...]), k).astype(BF16)

    u_ref[0] = _dot(h, wu_ref[...]).astype(BF16)
    ga_ref[0] = jax.nn.sigmoid(_dot(h, wga_ref[...])).astype(BF16)
    gb_ref[0] = jax.nn.sigmoid(_dot(h, wgb_ref[...])).astype(BF16)


def _attn_constants():
    pe = [[0.0] * ATTN_WIDTH for _ in range(LANES)]
    po = [[0.0] * ATTN_WIDTH for _ in range(LANES)]
    ones = [[0.0] * ATTN_WIDTH for _ in range(SUBLANES)]
    for pair in range(NUM_HEADS // 2):
        for i in range(BIAS_PIECES):
            pe[2 * pair + NUM_HEADS * i][LANES * pair + HEAD_DIM + i] = 1.0
            po[2 * pair + 1 + NUM_HEADS * i][LANES * pair + i] = 1.0
            ones[0][LANES * pair + HEAD_DIM + i] = 1.0
            ones[1][LANES * pair + i] = 1.0
    return jnp.array(pe, BF16), jnp.array(po, BF16), jnp.array(ones, F32)


def _inproj(x, g, wq, wk, wv, wf, bf, wu, wga, wgb):
    b, s, _ = x.shape
    tm = min(TM_PROJ, s)
    pe, po, ones = _attn_constants()
    tok = lambda width: pl.BlockSpec((1, tm, width), lambda i, j: (i, j, 0))
    act = lambda width: jax.ShapeDtypeStruct((b, s, width), BF16)
    consts = (g, wq, wk, wv, wf, bf, pe, po, ones, wu, wga, wgb)
    return pl.pallas_call(
        _inproj_kernel,
        grid=(b, s // tm),
        in_specs=[tok(D_MODEL)] + [_const_spec(c.shape) for c in consts],
        out_specs=[tok(ATTN_WIDTH)] * 6 + [tok(SSM_WIDTH), tok(D_MODEL), tok(D_MODEL)],
        out_shape=[act(ATTN_WIDTH)] * 6 + [act(SSM_WIDTH), act(D_MODEL), act(D_MODEL)],
        scratch_shapes=[pltpu.VMEM((SUBLANES, LANES), F32)],
        compiler_params=pltpu.CompilerParams(
            dimension_semantics=("arbitrary", "arbitrary"), vmem_limit_bytes=VMEM_LIMIT_BYTES),
        name="inproj",
    )(x, *consts)


def _attn_kernel(qe_ref, qo_ref, ke_ref, ko_ref, ve_ref, vo_ref, tri_ref, o_ref, m_ref, acc_ref):
    i = pl.program_id(1)
    tq = o_ref.shape[1]

    def block(j, row0, nrows, bias, init):
        rows = slice(row0, row0 + nrows)
        keys = pl.ds(pl.multiple_of(j * TK, TK), TK)
        for head in range(NUM_HEADS):
            ls = slice(LANES * (head // 2), LANES * (head // 2 + 1))
            q_ref, k_ref, v_ref = (qe_ref, ke_ref, ve_ref) if head % 2 == 0 else (qo_ref, ko_ref, vo_ref)
            s = lax.dot_general(q_ref[0, rows, ls], k_ref[0, keys, ls], (((1,), (1,)), ((), ())),
                                preferred_element_type=F32)
            if bias is not None:
                s = s + bias
            s_cols = [s[:, LANES * c:LANES * (c + 1)] for c in range(TK // LANES)]
            m_blk = jnp.max(functools.reduce(jnp.maximum, s_cols), axis=-1, keepdims=True)
            if init:
                m_new = jnp.broadcast_to(m_blk, (nrows, LANES))
            else:
                m_old = m_ref[head, rows, :]
                m_new = jnp.maximum(m_old, m_blk)
            p = jnp.concatenate([jnp.exp2((sc - m_new).astype(BF16)) for sc in s_cols], axis=1)
            pv = _dot(p, v_ref[0, keys, ls])
            if init:
                acc_ref[head, rows, :] = pv
            else:
                acc_ref[head, rows, :] = jnp.exp2(m_old - m_new) * acc_ref[head, rows, :] + pv
            m_ref[head, rows, :] = m_new

    blocks_per_step = tq // TK
    first_diag = i * blocks_per_step
    for d in range(blocks_per_step):
        for r in range(d, blocks_per_step):
            block(first_diag + d, r * TK, TK, tri_ref[...] if r == d else None, init=(d == 0))

    def body(j, carry):
        block(j, 0, tq, None, init=False)
        return carry

    lax.fori_loop(0, first_diag, body, 0)

    low = _low_half((tq, LANES))
    for pair in range(NUM_HEADS // 2):
        even = acc_ref[2 * pair]
        odd = acc_ref[2 * pair + 1]
        den = pltpu.roll(jnp.where(low, odd, even), HEAD_DIM, axis=1)
        o_ref[0, :, LANES * pair:LANES * (pair + 1)] = (jnp.where(low, even, odd) / den).astype(BF16)


def _attention(qe, qo, ke, ko, ve, vo):
    b, s, _ = qe.shape
    tq = min(TQ, s)
    assert tq % TK == 0 and s % tq == 0
    tri = jnp.where(lax.broadcasted_iota(jnp.int32, (TK, TK), 0)
                    >= lax.broadcasted_iota(jnp.int32, (TK, TK), 1), 0.0, MASK_VALUE).astype(F32)
    q_spec = pl.BlockSpec((1, tq, ATTN_WIDTH), lambda i, j: (i, j, 0))
    kv_spec = pl.BlockSpec((1, s, ATTN_WIDTH), lambda i, j: (i, 0, 0))
    return pl.pallas_call(
        _attn_kernel,
        grid=(b, s // tq),
        in_specs=[q_spec, q_spec, kv_spec, kv_spec, kv_spec, kv_spec, _const_spec(tri.shape)],
        out_specs=q_spec,
        out_shape=jax.ShapeDtypeStruct((b, s, ATTN_WIDTH), BF16),
        scratch_shapes=[pltpu.VMEM((NUM_HEADS, tq, LANES), F32),
                        pltpu.VMEM((NUM_HEADS, tq, LANES), F32)],
        compiler_params=pltpu.CompilerParams(
            dimension_semantics=("arbitrary", "arbitrary"), vmem_limit_bytes=VMEM_LIMIT_BYTES),
        name="attention",
    )(qe, qo, ke, ko, ve, vo, tri)


def _ssm_kernel(u_ref, bw_ref, ar_ref, ai_ref, cw_ref, d_ref, wglu_ref, bglu_ref, o_ref,
                st_ref, xr_ref, xi_ref):
    @pl.when(pl.program_id(1) == 0)
    def _():
        xr_ref[...] = jnp.zeros_like(xr_ref)
        xi_ref[...] = jnp.zeros_like(xi_ref)

    nb, lt, width = u_ref.shape
    rows = nb * lt
    half_w = width // 2
    half_s = N_STATES // 2
    slabs_half = half_s // LANES
    slabs_part = N_STATES // LANES
    u = u_ref[...].reshape(rows, width)

    for part in range(4):
        half = part % 2
        bu = _dot(u[:, half_w * half:half_w * (half + 1)], bw_ref[part])
        for s in range(slabs_half):
            slab = (part // 2) * slabs_part + half * slabs_half + s
            for b in range(nb):
                st_ref[slab, b * SSM_PITCH:b * SSM_PITCH + lt, :] = (
                    bu[b * lt:(b + 1) * lt, LANES * s:LANES * (s + 1)])

    for chunk in range(slabs_part // SCAN_SLABS):
        base = chunk * SCAN_SLABS
        cols = [slice(LANES * (base + s), LANES * (base + s + 1)) for s in range(SCAN_SLABS)]
        ar = [jnp.broadcast_to(ar_ref[:, c], (nb, LANES)) for c in cols]
        ai = [jnp.broadcast_to(ai_ref[:, c], (nb, LANES)) for c in cols]

        def step(t, carry, base=base, ar=ar, ai=ai):
            xr, xi = carry
            nr, ni = [], []
            for s in range(SCAN_SLABS):
                idx = pl.ds(t, nb, stride=SSM_PITCH)
                br = st_ref[base + s, idx, :]
                bi = st_ref[slabs_part + base + s, idx, :]
                r = ar[s] * xr[s] - ai[s] * xi[s] + br
                m = ar[s] * xi[s] + ai[s] * xr[s] + bi
                st_ref[base + s, idx, :] = r
                st_ref[slabs_part + base + s, idx, :] = m
                nr.append(r)
                ni.append(m)
            return tuple(nr), tuple(ni)

        init = (tuple(xr_ref[:, c] for c in cols), tuple(xi_ref[:, c] for c in cols))
        xr, xi = lax.fori_loop(0, lt, step, init)
        for s, c in enumerate(cols):
            xr_ref[:, c] = xr[s]
            xi_ref[:, c] = xi[s]

    def states(slab):
        return jnp.concatenate(
            [st_ref[slab, b * SSM_PITCH:b * SSM_PITCH + lt, :] for b in range(nb)], axis=0).astype(BF16)

    ys = []
    for half in range(2):
        x_re = jnp.concatenate([states(half * slabs_half + s) for s in range(slabs_half)], axis=1)
        x_im = jnp.concatenate(
            [states(slabs_part + half * slabs_half + s) for s in range(slabs_half)], axis=1)
        ys.append(_dot(x_re, cw_ref[half]) + _dot(x_im, cw_ref[2 + half]))
    y = jnp.concatenate(ys, axis=1) + d_ref[...] * u.astype(F32)

    y = jax.nn.gelu(y, approximate=True)
    y = y * jax.nn.sigmoid(_dot(y.astype(BF16), wglu_ref[...]) + bglu_ref[...])
    o_ref[...] = y.reshape(nb, lt, width).astype(BF16)


def _ssm(u, bw, ar, ai, cw, d, wglu, bglu):
    b, s, _ = u.shape
    assert b % SSM_BATCH == 0 and s % LT == 0
    tok = pl.BlockSpec((SSM_BATCH, LT, SSM_WIDTH), lambda i, j: (i, j, 0))
    return pl.pallas_call(
        _ssm_kernel,
        grid=(b // SSM_BATCH, s // LT),
        in_specs=[tok, _const_spec(bw.shape), _const_spec(ar.shape), _const_spec(ai.shape),
                  _const_spec(cw.shape), _const_spec(d.shape), _const_spec(wglu.shape),
                  _const_spec(bglu.shape)],
        out_specs=tok,
        out_shape=jax.ShapeDtypeStruct(u.shape, BF16),
        scratch_shapes=[pltpu.VMEM((2 * N_STATES // LANES, SSM_BATCH * SSM_PITCH, LANES), F32),
                        pltpu.VMEM((SSM_BATCH, N_STATES), F32),
                        pltpu.VMEM((SSM_BATCH, N_STATES), F32)],
        compiler_params=pltpu.CompilerParams(
            dimension_semantics=("arbitrary", "arbitrary"), vmem_limit_bytes=VMEM_LIMIT_BYTES),
        name="s5",
    )(u, bw, ar, ai, cw, d, wglu, bglu)


def _merge_mlp_kernel(x_ref, ya_ref, yb_ref, ga_ref, gb_ref, wa_ref, wb_ref, wo_ref, g_ref,
                      wup_ref, wdn_ref, gfin_ref, o_ref, *, final_norm):
    mixed = (ga_ref[...].astype(F32) * _dot(ya_ref[...], wa_ref[...])
             + gb_ref[...].astype(F32) * _dot(yb_ref[...], wb_ref[...]))
    x = x_ref[...] + _dot(mixed.astype(BF16), wo_ref[...])
    h = _rmsnorm(x, g_ref[...]).astype(BF16)
    for c in range(D_FF // FF_CHUNK):
        cs = slice(c * FF_CHUNK, (c + 1) * FF_CHUNK)
        up = jnp.maximum(_dot(h, wup_ref[:, cs]), 0.0)
        x = x + _dot((up * up).astype(BF16), wdn_ref[cs, :])
    if final_norm:
        x = _rmsnorm(x, gfin_ref[...])
    o_ref[...] = x


def _merge_mlp(x, ya, yb, ga, gb, wa, wb, wo, g, wup, wdn, gfin, final_norm):
    t = x.shape[0]
    tm = min(TM_MLP, t)
    tok = lambda width: pl.BlockSpec((tm, width), lambda i: (i, 0))
    return pl.pallas_call(
        functools.partial(_merge_mlp_kernel, final_norm=final_norm),
        grid=(t // tm,),
        in_specs=[tok(D_MODEL), tok(ATTN_WIDTH), tok(SSM_WIDTH), tok(D_MODEL), tok(D_MODEL),
                  _const_spec(wa.shape), _const_spec(wb.shape), _const_spec(wo.shape),
                  _const_spec(g.shape), _const_spec(wup.shape), _const_spec(wdn.shape),
                  _const_spec(gfin.shape)],
        out_specs=tok(D_MODEL),
        out_shape=jax.ShapeDtypeStruct(x.shape, F32),
        compiler_params=pltpu.CompilerParams(
            dimension_semantics=("arbitrary",), vmem_limit_bytes=VMEM_LIMIT_BYTES),
        name="merge_mlp",
    )(x, ya, yb, ga, gb, wa, wb, wo, g, wup, wdn, gfin)


def _ssm_params(lam_re, lam_im, log_dt, b_re, b_im, c_re, c_im):
    dt = jnp.exp(log_dt)[:, None]
    mag = jnp.exp(lam_re * dt)
    ar = mag * jnp.cos(lam_im * dt)
    ai = mag * jnp.sin(lam_im * dt)
    den = lam_re * lam_re + lam_im * lam_im
    zr = ((ar - 1.0) * lam_re + ai * lam_im) / den
    zi = (ai * lam_re - (ar - 1.0) * lam_im) / den
    bb_re = zr[:, :, None] * b_re - zi[:, :, None] * b_im
    bb_im = zr[:, :, None] * b_im + zi[:, :, None] * b_re

    hg = SSM_GROUPS // 2
    eye = jnp.eye(hg, dtype=F32)

    def b_block(m):
        return jnp.einsum("gpc,gh->gchp", m, eye).reshape(hg * SSM_GROUP_CH, hg * SSM_STATE)

    def c_block(m):
        return jnp.einsum("gcp,gh->gphc", m, eye).reshape(hg * SSM_STATE, hg * SSM_GROUP_CH)

    bw = jnp.stack([b_block(bb_re[:hg]), b_block(bb_re[hg:]), b_block(bb_im[:hg]), b_block(bb_im[hg:])])
    cw = jnp.stack([c_block(c_re[:hg]), c_block(c_re[hg:]), c_block(-c_im[:hg]), c_block(-c_im[hg:])])
    return bw.astype(BF16), ar.reshape(1, N_STATES), ai.reshape(1, N_STATES), cw.astype(BF16)


def _split_w_in(w_in, b_forget):
    o1 = ATTN_WIDTH
    o2 = o1 + ATTN_WIDTH
    o3 = o2 + ATTN_WIDTH
    o4 = o3 + NUM_HEADS
    o5 = o4 + SSM_WIDTH
    o6 = o5 + D_MODEL
    wq = (w_in[:, :o1] * (LOG2E * HEAD_DIM ** -0.5)).astype(BF16)
    wf = jnp.zeros((D_MODEL, LANES), F32).at[:, :BIAS_PIECES * NUM_HEADS].set(
        jnp.tile(w_in[:, o3:o4], (1, BIAS_PIECES))).astype(BF16)
    bf = jnp.zeros((1, LANES), F32).at[0, :BIAS_PIECES * NUM_HEADS].set(jnp.tile(b_forget, BIAS_PIECES))
    return (wq, w_in[:, o1:o2].astype(BF16), w_in[:, o2:o3].astype(BF16), wf, bf,
            w_in[:, o4:o5].astype(BF16), w_in[:, o5:o6].astype(BF16), w_in[:, o6:].astype(BF16))


@jax.jit
def kernel(x, norm_mix, w_in, b_forget, ssm_lambda_re, ssm_lambda_im, ssm_log_dt, ssm_b_re, ssm_b_im,
           ssm_c_re, ssm_c_im, ssm_d, w_glu, b_glu, w_branch_a, w_branch_b, w_out, norm_mlp,
           w_mlp_up, w_mlp_down, norm_final):
    b, s, d = x.shape
    depth = w_in.shape[0]
    gfin = norm_final.reshape(1, d)
    for l in range(depth):
        wq, wk, wv, wf, bf, wu, wga, wgb = _split_w_in(w_in[l], b_forget[l])
        qe, qo, ke, ko, ve, vo, u, ga, gb = _inproj(
            x, norm_mix[l].reshape(1, d), wq, wk, wv, wf, bf, wu, wga, wgb)
        ya = _attention(qe, qo, ke, ko, ve, vo)
        bw, ar, ai, cw = _ssm_params(ssm_lambda_re[l], ssm_lambda_im[l], ssm_log_dt[l],
                                     ssm_b_re[l], ssm_b_im[l], ssm_c_re[l], ssm_c_im[l])
        yb = _ssm(u, bw, ar, ai, cw, ssm_d[l].reshape(1, SSM_WIDTH), w_glu[l].astype(BF16),
                  b_glu[l].reshape(1, SSM_WIDTH))
        t = b * s
        x = _merge_mlp(x.reshape(t, d), ya.reshape(t, ATTN_WIDTH), yb.reshape(t, SSM_WIDTH),
                       ga.reshape(t, d), gb.reshape(t, d), w_branch_a[l].astype(BF16),
                       w_branch_b[l].astype(BF16), w_out[l].astype(BF16), norm_mlp[l].reshape(1, d),
                       w_mlp_up[l].astype(BF16), w_mlp_down[l].astype(BF16), gfin,
                       final_norm=(l == depth - 1)).reshape(b, s, d)
    return x
```

```python
import functools
import math

import jax
import jax.numpy as jnp
from jax import lax
from jax.experimental import pallas as pl
from jax.experimental.pallas import tpu as pltpu

F32 = jnp.float32
BF16 = jnp.bfloat16

D_MODEL = 1024
NUM_HEADS = 8
HEAD_DIM = 64
ATTN_WIDTH = NUM_HEADS * HEAD_DIM
SSM_GROUPS = 32
SSM_GROUP_CH = 16
SSM_STATE = 64
SSM_WIDTH = SSM_GROUPS * SSM_GROUP_CH
N_STATES = SSM_GROUPS * SSM_STATE
D_FF = 4 * D_MODEL
RMS_EPS = 1e-6
MASK_VALUE = -1e30
LOG2E = math.log2(math.e)
BIAS_PIECES = 3

LANES = 128
SUBLANES = 8
VMEM_LIMIT_BYTES = 56 * 1024 * 1024

TM_PROJ = 512
TQ = 512
TK = 256
LT = 64
SSM_BATCH = SUBLANES
SSM_PITCH = LT + SUBLANES
SCAN_SLABS = 8
TM_MLP = 512
FF_CHUNK = 1024

_dot = functools.partial(jnp.dot, preferred_element_type=F32)


def _rmsnorm(x, g):
    ms = jnp.mean(x * x, axis=-1, keepdims=True)
    return x * lax.rsqrt(ms + RMS_EPS) * g


def _const_spec(shape):
    zeros = (0,) * len(shape)
    return pl.BlockSpec(shape, lambda *_: zeros, pipeline_mode=pl.Buffered(1))


def _low_half(shape):
    lane = lax.broadcasted_iota(jnp.int32, shape, len(shape) - 1)
    return (lane & (LANES - 1)) < HEAD_DIM


def _cumsum_rows(x):
    n = x.shape[0]
    row = lax.broadcasted_iota(jnp.int32, x.shape, 0)
    k = 1
    while k < n:
        x = x + jnp.where(row >= k, pltpu.roll(x, k, axis=0), 0.0)
        k *= 2
    return x


def _inproj_kernel(x_ref, g_ref, wq_ref, wk_ref, wv_ref, wf_ref, bf_ref, pe_ref, po_ref, one_ref,
                   wu_ref, wga_ref, wgb_ref,
                   qe_ref, qo_ref, ke_ref, ko_ref, ve_ref, vo_ref, u_ref, ga_ref, gb_ref, carry_ref):
    @pl.when(pl.program_id(1) == 0)
    def _():
        carry_ref[...] = jnp.zeros_like(carry_ref)

    tm = x_ref.shape[1]
    h = _rmsnorm(x_ref[0], g_ref[...]).astype(BF16)
    low = _low_half((tm, ATTN_WIDTH))
    q_ones = one_ref[0:1, :]
    q_ones_odd = one_ref[1:2, :]

    q = _dot(h, wq_ref[...])
    qe_ref[0] = jnp.where(low, q, q_ones).astype(BF16)
    qo_ref[0] = jnp.where(low, q_ones_odd, q).astype(BF16)
    v = _dot(h, wv_ref[...])
    ve_ref[0] = jnp.where(low, v, 1.0).astype(BF16)
    vo_ref[0] = jnp.where(low, 1.0, v).astype(BF16)

    fl = _dot(h, wf_ref[...]) + bf_ref[...]
    log_f = jnp.minimum(fl, 0.0) - jnp.log1p(jnp.exp(-jnp.abs(fl)))
    cum = _cumsum_rows(log_f) + carry_ref[0:1, :]
    carry_ref[...] = jnp.broadcast_to(cum[tm - 1:tm, :], carry_ref.shape)
    bias = cum * (-LOG2E)
    hi = bias.astype(BF16).astype(F32)
    mid = (bias - hi).astype(BF16).astype(F32)
    lo = (bias - hi - mid).astype(BF16).astype(F32)
    lane = lax.broadcasted_iota(jnp.int32, bias.shape, 1)
    pieces = jnp.where(lane < NUM_HEADS, hi, jnp.where(lane < 2 * NUM_HEADS, mid, lo)).astype(BF16)
    k = _dot(h, wk_ref[...])
    ke_ref[0] = jnp.where(low, k, _dot(pieces, pe_ref[...])).astype(BF16)
    ko_ref[0] = jnp.where(low, _dot(pieces, po_ref[...]), k).astype(BF16)

    u_ref[0] = _dot(h, wu_ref[...]).astype(BF16)
    ga_ref[0] = jax.nn.sigmoid(_dot(h, wga_ref[...])).astype(BF16)
    gb_ref[0] = jax.nn.sigmoid(_dot(h, wgb_ref[...])).astype(BF16)


def _attn_constants():
    pe = [[0.0] * ATTN_WIDTH for _ in range(LANES)]
    po = [[0.0] * ATTN_WIDTH for _ in range(LANES)]
    ones = [[0.0] * ATTN_WIDTH for _ in range(SUBLANES)]
    for pair in range(NUM_HEADS // 2):
        for i in range(BIAS_PIECES):
            pe[2 * pair + NUM_HEADS * i][LANES * pair + HEAD_DIM + i] = 1.0
            po[2 * pair + 1 + NUM_HEADS * i][LANES * pair + i] = 1.0
            ones[0][LANES * pair + HEAD_DIM + i] = 1.0
            ones[1][LANES * pair + i] = 1.0
    return jnp.array(pe, BF16), jnp.array(po, BF16), jnp.array(ones, F32)


def _inproj(x, g, wq, wk, wv, wf, bf, wu, wga, wgb):
    b, s, _ = x.shape
    tm = min(TM_PROJ, s)
    pe, po, ones = _attn_constants()
    tok = lambda width: pl.BlockSpec((1, tm, width), lambda i, j: (i, j, 0))
    act = lambda width: jax.ShapeDtypeStruct((b, s, width), BF16)
    consts = (g, wq, wk, wv, wf, bf, pe, po, ones, wu, wga, wgb)
    return pl.pallas_call(
        _inproj_kernel,
        grid=(b, s // tm),
        in_specs=[tok(D_MODEL)] + [_const_spec(c.shape) for c in consts],
        out_specs=[tok(ATTN_WIDTH)] * 6 + [tok(SSM_WIDTH), tok(D_MODEL), tok(D_MODEL)],
        out_shape=[act(ATTN_WIDTH)] * 6 + [act(SSM_WIDTH), act(D_MODEL), act(D_MODEL)],
        scratch_shapes=[pltpu.VMEM((SUBLANES, LANES), F32)],
        compiler_params=pltpu.CompilerParams(
            dimension_semantics=("arbitrary", "arbitrary"), vmem_limit_bytes=VMEM_LIMIT_BYTES),
        name="inproj",
    )(x, *consts)


def _attn_kernel(qe_ref, qo_ref, ke_ref, ko_ref, ve_ref, vo_ref, tri_ref, o_ref, m_ref, acc_ref):
    i = pl.program_id(1)
    tq = o_ref.shape[1]

    def block(j, bias, init):
        keys = pl.ds(pl.multiple_of(j * TK, TK), TK)
        for head in range(NUM_HEADS):
            ls = slice(LANES * (head // 2), LANES * (head // 2 + 1))
            q_ref, k_ref, v_ref = (qe_ref, ke_ref, ve_ref) if head % 2 == 0 else (qo_ref, ko_ref, vo_ref)
            s = lax.dot_general(q_ref[0, :, ls], k_ref[0, keys, ls], (((1,), (1,)), ((), ())),
                                preferred_element_type=F32)
            if bias is not None:
                s = s + bias
            s_cols = [s[:, LANES * c:LANES * (c + 1)] for c in range(TK // LANES)]
            m_blk = jnp.max(functools.reduce(jnp.maximum, s_cols), axis=-1, keepdims=True)
            if init:
                m_new = jnp.broadcast_to(m_blk, (tq, LANES))
            else:
                m_old = m_ref[head]
                m_new = jnp.maximum(m_old, m_blk)
            p = jnp.concatenate([jnp.exp2((sc - m_new).astype(BF16)) for sc in s_cols], axis=1)
            pv = _dot(p, v_ref[0, keys, ls])
            if init:
                acc_ref[head] = pv
            else:
                acc_ref[head] = jnp.exp2(m_old - m_new) * acc_ref[head] + pv
            m_ref[head] = m_new

    blocks_per_step = tq // TK
    for c in range(blocks_per_step):
        block(i * blocks_per_step + c, tri_ref[:, TK * c:TK * (c + 1)], init=(c == 0))

    def body(jj, carry):
        for c in range(blocks_per_step):
            block(jj * blocks_per_step + c, None, init=False)
        return carry

    lax.fori_loop(0, i, body, 0)

    low = _low_half((tq, LANES))
    for pair in range(NUM_HEADS // 2):
        even = acc_ref[2 * pair]
        odd = acc_ref[2 * pair + 1]
        den = pltpu.roll(jnp.where(low, odd, even), HEAD_DIM, axis=1)
        o_ref[0, :, LANES * pair:LANES * (pair + 1)] = (jnp.where(low, even, odd) / den).astype(BF16)


def _attention(qe, qo, ke, ko, ve, vo):
    b, s, _ = qe.shape
    tq = min(TQ, s)
    assert tq % TK == 0 and s % tq == 0
    tri = jnp.where(lax.broadcasted_iota(jnp.int32, (tq, tq), 0)
                    >= lax.broadcasted_iota(jnp.int32, (tq, tq), 1), 0.0, MASK_VALUE).astype(F32)
    q_spec = pl.BlockSpec((1, tq, ATTN_WIDTH), lambda i, j: (i, j, 0))
    kv_spec = pl.BlockSpec((1, s, ATTN_WIDTH), lambda i, j: (i, 0, 0))
    return pl.pallas_call(
        _attn_kernel,
        grid=(b, s // tq),
        in_specs=[q_spec, q_spec, kv_spec, kv_spec, kv_spec, kv_spec, _const_spec(tri.shape)],
        out_specs=q_spec,
        out_shape=jax.ShapeDtypeStruct((b, s, ATTN_WIDTH), BF16),
        scratch_shapes=[pltpu.VMEM((NUM_HEADS, tq, LANES), F32),
                        pltpu.VMEM((NUM_HEADS, tq, LANES), F32)],
        compiler_params=pltpu.CompilerParams(
            dimension_semantics=("arbitrary", "arbitrary"), vmem_limit_bytes=VMEM_LIMIT_BYTES),
        name="attention",
    )(qe, qo, ke, ko, ve, vo, tri)


def _ssm_kernel(u_ref, un_ref, bw_ref, ar_ref, ai_ref, cw_ref, d_ref, wglu_ref, bglu_ref, o_ref,
                st0_ref, st1_ref, ub0_ref, ub1_ref, pin0_ref, pin1_ref, pout_ref, xr_ref, xi_ref):
    nb, lt2, width = u_ref.shape
    lt = lt2 // 2
    half_w = width // 2
    half_s = N_STATES // 2
    n_slab = width // LANES

    def to_time_major(u, perm_ref):
        uf = u.astype(F32)
        for s in range(n_slab):
            for b in range(nb):
                perm_ref[s, b * SSM_PITCH:b * SSM_PITCH + lt, :] = uf[b, :, LANES * s:LANES * (s + 1)]
        groups = [jnp.concatenate([perm_ref[s, pl.ds(t, nb, stride=SSM_PITCH), :] for s in range(n_slab)],
                                  axis=1) for t in range(lt)]
        return jnp.concatenate(groups, axis=0).astype(BF16)

    def store_seq_major(y, t0):
        for t in range(lt):
            for s in range(n_slab):
                pout_ref[s, pl.ds(t, nb, stride=SSM_PITCH), :] = (
                    y[nb * t:nb * (t + 1), LANES * s:LANES * (s + 1)])
        for b in range(nb):
            o_ref[b, t0:t0 + lt, :] = jnp.concatenate(
                [pout_ref[s, b * SSM_PITCH:b * SSM_PITCH + lt, :] for s in range(n_slab)],
                axis=1).astype(BF16)

    def project_in(ub_ref, st_ref):
        for part in range(4):
            half = part % 2
            st_ref[:, half_s * part:half_s * (part + 1)] = _dot(
                ub_ref[:, half_w * half:half_w * (half + 1)], bw_ref[part])

    def scan(st_ref):
        for chunk in range(N_STATES // LANES // SCAN_SLABS):
            cols = [slice(LANES * (chunk * SCAN_SLABS + s), LANES * (chunk * SCAN_SLABS + s + 1))
                    for s in range(SCAN_SLABS)]
            icols = [slice(N_STATES + c.start, N_STATES + c.stop) for c in cols]
            ar = [jnp.broadcast_to(ar_ref[:, c], (nb, LANES)) for c in cols]
            ai = [jnp.broadcast_to(ai_ref[:, c], (nb, LANES)) for c in cols]
            xr = [xr_ref[:, c] for c in cols]
            xi = [xi_ref[:, c] for c in cols]
            for t in range(lt):
                rows = slice(nb * t, nb * (t + 1))
                for s in range(SCAN_SLABS):
                    r = ar[s] * xr[s] - ai[s] * xi[s] + st_ref[rows, cols[s]]
                    m = ar[s] * xi[s] + ai[s] * xr[s] + st_ref[rows, icols[s]]
                    st_ref[rows, cols[s]] = r
                    st_ref[rows, icols[s]] = m
                    xr[s], xi[s] = r, m
            for s, c in enumerate(cols):
                xr_ref[:, c] = xr[s]
                xi_ref[:, c] = xi[s]

    def project_out(st_ref, ub_ref):
        ys = []
        for half in range(2):
            x_re = st_ref[:, half_s * half:half_s * (half + 1)].astype(BF16)
            x_im = st_ref[:, half_s * (2 + half):half_s * (3 + half)].astype(BF16)
            ys.append(_dot(x_re, cw_ref[half]) + _dot(x_im, cw_ref[2 + half]))
        y = jnp.concatenate(ys, axis=1) + d_ref[...] * ub_ref[...].astype(F32)
        y = jax.nn.gelu(y, approximate=True)
        return y * jax.nn.sigmoid(_dot(y.astype(BF16), wglu_ref[...]) + bglu_ref[...])

    @pl.when(pl.program_id(1) == 0)
    def _():
        xr_ref[...] = jnp.zeros_like(xr_ref)
        xi_ref[...] = jnp.zeros_like(xi_ref)
        ub0_ref[...] = to_time_major(u_ref[:, 0:lt, :], pin0_ref)
        project_in(ub0_ref, st0_ref)

    ub1_ref[...] = to_time_major(u_ref[:, lt:lt2, :], pin1_ref)
    project_in(ub1_ref, st1_ref)
    scan(st0_ref)
    store_seq_major(project_out(st0_ref, ub0_ref), 0)
    ub0_ref[...] = to_time_major(un_ref[...], pin0_ref)
    project_in(ub0_ref, st0_ref)
    scan(st1_ref)
    store_seq_major(project_out(st1_ref, ub1_ref), lt)


def _ssm(u, bw, ar, ai, cw, d, wglu, bglu):
    b, s, _ = u.shape
    assert b % SSM_BATCH == 0 and s % (2 * LT) == 0
    last_tile = s // LT - 1
    rows = SSM_BATCH * LT
    tok = pl.BlockSpec((SSM_BATCH, 2 * LT, SSM_WIDTH), lambda i, j: (i, j, 0))
    nxt = pl.BlockSpec((SSM_BATCH, LT, SSM_WIDTH), lambda i, j: (i, jnp.minimum(2 * j + 2, last_tile), 0))
    state_buf = pltpu.VMEM((rows, 2 * N_STATES), F32)
    u_buf = pltpu.VMEM((rows, SSM_WIDTH), BF16)
    perm_buf = pltpu.VMEM((SSM_WIDTH // LANES, SSM_BATCH * SSM_PITCH, LANES), F32)
    return pl.pallas_call(
        _ssm_kernel,
        grid=(b // SSM_BATCH, s // (2 * LT)),
        in_specs=[tok, nxt, _const_spec(bw.shape), _const_spec(ar.shape), _const_spec(ai.shape),
                  _const_spec(cw.shape), _const_spec(d.shape), _const_spec(wglu.shape),
                  _const_spec(bglu.shape)],
        out_specs=tok,
        out_shape=jax.ShapeDtypeStruct(u.shape, BF16),
        scratch_shapes=[state_buf, state_buf, u_buf, u_buf, perm_buf, perm_buf, perm_buf,
                        pltpu.VMEM((SSM_BATCH, N_STATES), F32),
                        pltpu.VMEM((SSM_BATCH, N_STATES), F32)],
        compiler_params=pltpu.CompilerParams(
            dimension_semantics=("arbitrary", "arbitrary"), vmem_limit_bytes=VMEM_LIMIT_BYTES),
        name="s5",
    )(u, u, bw, ar, ai, cw, d, wglu, bglu)


def _merge_mlp_kernel(x_ref, ya_ref, yb_ref, ga_ref, gb_ref, wa_ref, wb_ref, wo_ref, g_ref,
                      wup_ref, wdn_ref, gfin_ref, o_ref, *, final_norm):
    mixed = (ga_ref[...].astype(F32) * _dot(ya_ref[...], wa_ref[...])
             + gb_ref[...].astype(F32) * _dot(yb_ref[...], wb_ref[...]))
    x = x_ref[...] + _dot(mixed.astype(BF16), wo_ref[...])
    h = _rmsnorm(x, g_ref[...]).astype(BF16)
    for c in range(D_FF // FF_CHUNK):
        cs = slice(c * FF_CHUNK, (c + 1) * FF_CHUNK)
        up = jnp.maximum(_dot(h, wup_ref[:, cs]), 0.0)
        x = x + _dot((up * up).astype(BF16), wdn_ref[cs, :])
    if final_norm:
        x = _rmsnorm(x, gfin_ref[...])
    o_ref[...] = x


def _merge_mlp(x, ya, yb, ga, gb, wa, wb, wo, g, wup, wdn, gfin, final_norm):
    t = x.shape[0]
    tm = min(TM_MLP, t)
    tok = lambda width: pl.BlockSpec((tm, width), lambda i: (i, 0))
    return pl.pallas_call(
        functools.partial(_merge_mlp_kernel, final_norm=final_norm),
        grid=(t // tm,),
        in_specs=[tok(D_MODEL), tok(ATTN_WIDTH), tok(SSM_WIDTH), tok(D_MODEL), tok(D_MODEL),
                  _const_spec(wa.shape), _const_spec(wb.shape), _const_spec(wo.shape),
                  _const_spec(g.shape), _const_spec(wup.shape), _const_spec(wdn.shape),
                  _const_spec(gfin.shape)],
        out_specs=tok(D_MODEL),
        out_shape=jax.ShapeDtypeStruct(x.shape, F32),
        compiler_params=pltpu.CompilerParams(
            dimension_semantics=("arbitrary",), vmem_limit_bytes=VMEM_LIMIT_BYTES),
        name="merge_mlp",
    )(x, ya, yb, ga, gb, wa, wb, wo, g, wup, wdn, gfin)


def _ssm_params(lam_re, lam_im, log_dt, b_re, b_im, c_re, c_im):
    dt = jnp.exp(log_dt)[:, None]
    mag = jnp.exp(lam_re * dt)
    ar = mag * jnp.cos(lam_im * dt)
    ai = mag * jnp.sin(lam_im * dt)
    den = lam_re * lam_re + lam_im * lam_im
    zr = ((ar - 1.0) * lam_re + ai * lam_im) / den
    zi = (ai * lam_re - (ar - 1.0) * lam_im) / den
    bb_re = zr[:, :, None] * b_re - zi[:, :, None] * b_im
    bb_im = zr[:, :, None] * b_im + zi[:, :, None] * b_re

    hg = SSM_GROUPS // 2
    eye = jnp.eye(hg, dtype=F32)

    def b_block(m):
        return jnp.einsum("gpc,gh->gchp", m, eye).reshape(hg * SSM_GROUP_CH, hg * SSM_STATE)

    def c_block(m):
        return jnp.einsum("gcp,gh->gphc", m, eye).reshape(hg * SSM_STATE, hg * SSM_GROUP_CH)

    bw = jnp.stack([b_block(bb_re[:hg]), b_block(bb_re[hg:]), b_block(bb_im[:hg]), b_block(bb_im[hg:])])
    cw = jnp.stack([c_block(c_re[:hg]), c_block(c_re[hg:]), c_block(-c_im[:hg]), c_block(-c_im[hg:])])
    return bw.astype(BF16), ar.reshape(1, N_STATES), ai.reshape(1, N_STATES), cw.astype(BF16)


def _split_w_in(w_in, b_forget):
    o1 = ATTN_WIDTH
    o2 = o1 + ATTN_WIDTH
    o3 = o2 + ATTN_WIDTH
    o4 = o3 + NUM_HEADS
    o5 = o4 + SSM_WIDTH
    o6 = o5 + D_MODEL
    wq = (w_in[:, :o1] * (LOG2E * HEAD_DIM ** -0.5)).astype(BF16)
    wf = jnp.zeros((D_MODEL, LANES), F32).at[:, :BIAS_PIECES * NUM_HEADS].set(
        jnp.tile(w_in[:, o3:o4], (1, BIAS_PIECES))).astype(BF16)
    bf = jnp.zeros((1, LANES), F32).at[0, :BIAS_PIECES * NUM_HEADS].set(jnp.tile(b_forget, BIAS_PIECES))
    return (wq, w_in[:, o1:o2].astype(BF16), w_in[:, o2:o3].astype(BF16), wf, bf,
            w_in[:, o4:o5].astype(BF16), w_in[:, o5:o6].astype(BF16), w_in[:, o6:].astype(BF16))


@jax.jit
def kernel(x, norm_mix, w_in, b_forget, ssm_lambda_re, ssm_lambda_im, ssm_log_dt, ssm_b_re, ssm_b_im,
           ssm_c_re, ssm_c_im, ssm_d, w_glu, b_glu, w_branch_a, w_branch_b, w_out, norm_mlp,
           w_mlp_up, w_mlp_down, norm_final):
    b, s, d = x.shape
    depth = w_in.shape[0]
    gfin = norm_final.reshape(1, d)
    for l in range(depth):
        wq, wk, wv, wf, bf, wu, wga, wgb = _split_w_in(w_in[l], b_forget[l])
        qe, qo, ke, ko, ve, vo, u, ga, gb = _inproj(
            x, norm_mix[l].reshape(1, d), wq, wk, wv, wf, bf, wu, wga, wgb)
        ya = _attention(qe, qo, ke, ko, ve, vo)
        bw, ar, ai, cw = _ssm_params(ssm_lambda_re[l], ssm_lambda_im[l], ssm_log_dt[l],
                                     ssm_b_re[l], ssm_b_im[l], ssm_c_re[l], ssm_c_im[l])
        yb = _ssm(u, bw, ar, ai, cw, ssm_d[l].reshape(1, SSM_WIDTH), w_glu[l].astype(BF16),
                  b_glu[l].reshape(1, SSM_WIDTH))
        t = b * s
        x = _merge_mlp(x.reshape(t, d), ya.reshape(t, ATTN_WIDTH), yb.reshape(t, SSM_WIDTH),
                       ga.reshape(t, d), gb.reshape(t, d), w_branch_a[l].astype(BF16),
                       w_branch_b[l].astype(BF16), w_out[l].astype(BF16), norm_mlp[l].reshape(1, d),
                       w_mlp_up[l].astype(BF16), w_mlp_down[l].astype(BF16), gfin,
                       final_norm=(l == depth - 1)).reshape(b, s, d)
    return x
```

```python
import functools
import math

import jax
import jax.numpy as jnp
from jax import lax
from jax.experimental import pallas as pl
from jax.experimental.pallas import tpu as pltpu

F32 = jnp.float32
BF16 = jnp.bfloat16

D_MODEL = 1024
NUM_HEADS = 8
HEAD_DIM = 64
ATTN_WIDTH = NUM_HEADS * HEAD_DIM
SSM_GROUPS = 32
SSM_GROUP_CH = 16
SSM_STATE = 64
SSM_WIDTH = SSM_GROUPS * SSM_GROUP_CH
N_STATES = SSM_GROUPS * SSM_STATE
D_FF = 4 * D_MODEL
RMS_EPS = 1e-6
MASK_VALUE = -1e30
LOG2E = math.log2(math.e)
BIAS_PIECES = 3

LANES = 128
SUBLANES = 8
VMEM_LIMIT_BYTES = 56 * 1024 * 1024

TM_PROJ = 512
TQ = 512
TK = 256
LT = 64
SSM_BATCH = SUBLANES
SSM_PITCH = LT + SUBLANES
SCAN_SLABS = 8
TM_MLP = 512
FF_CHUNK = 1024

_dot = functools.partial(jnp.dot, preferred_element_type=F32)


def _rmsnorm(x, g):
    ms = jnp.mean(x * x, axis=-1, keepdims=True)
    return x * lax.rsqrt(ms + RMS_EPS) * g


def _const_spec(shape, layer=None):
    if layer is None:
        zeros = (0,) * len(shape)
        return pl.BlockSpec(shape, lambda *_: zeros, pipeline_mode=pl.Buffered(1))
    index = (layer,) + (0,) * (len(shape) - 1)
    return pl.BlockSpec((None,) + tuple(shape[1:]), lambda *_: index, pipeline_mode=pl.Buffered(1))


def _low_half(shape):
    lane = lax.broadcasted_iota(jnp.int32, shape, len(shape) - 1)
    return (lane & (LANES - 1)) < HEAD_DIM


def _cumsum_rows(x):
    n = x.shape[0]
    row = lax.broadcasted_iota(jnp.int32, x.shape, 0)
    k = 1
    while k < n:
        x = x + jnp.where(row >= k, pltpu.roll(x, k, axis=0), 0.0)
        k *= 2
    return x


def _inproj_kernel(x_ref, g_ref, wq_ref, wk_ref, wv_ref, wf_ref, bf_ref, pe_ref, po_ref, one_ref,
                   wu_ref, wga_ref, wgb_ref,
                   qe_ref, qo_ref, ke_ref, ko_ref, ve_ref, vo_ref, u_ref, ga_ref, gb_ref, carry_ref):
    @pl.when(pl.program_id(1) == 0)
    def _():
        carry_ref[...] = jnp.zeros_like(carry_ref)

    tm = x_ref.shape[1]
    h = _rmsnorm(x_ref[0], g_ref[...]).astype(BF16)
    low = _low_half((tm, ATTN_WIDTH))
    q_ones = one_ref[0:1, :]
    q_ones_odd = one_ref[1:2, :]

    q = _dot(h, wq_ref[...])
    qe_ref[0] = jnp.where(low, q, q_ones).astype(BF16)
    qo_ref[0] = jnp.where(low, q_ones_odd, q).astype(BF16)
    v = _dot(h, wv_ref[...])
    ve_ref[0] = jnp.where(low, v, 1.0).astype(BF16)
    vo_ref[0] = jnp.where(low, 1.0, v).astype(BF16)

    fl = _dot(h, wf_ref[...]) + bf_ref[...]
    log_f = jnp.minimum(fl, 0.0) - jnp.log1p(jnp.exp(-jnp.abs(fl)))
    cum = _cumsum_rows(log_f) + carry_ref[0:1, :]
    carry_ref[...] = jnp.broadcast_to(cum[tm - 1:tm, :], carry_ref.shape)
    bias = cum * (-LOG2E)
    hi = bias.astype(BF16).astype(F32)
    mid = (bias - hi).astype(BF16).astype(F32)
    lo = (bias - hi - mid).astype(BF16).astype(F32)
    lane = lax.broadcasted_iota(jnp.int32, bias.shape, 1)
    pieces = jnp.where(lane < NUM_HEADS, hi, jnp.where(lane < 2 * NUM_HEADS, mid, lo)).astype(BF16)
    k = _dot(h, wk_ref[...])
    ke_ref[0] = jnp.where(low, k, _dot(pieces, pe_ref[...])).astype(BF16)
    ko_ref[0] = jnp.where(low, _dot(pieces, po_ref[...]), k).astype(BF16)

    u_ref[0] = _dot(h, wu_ref[...]).astype(BF16)
    ga_ref[0] = jax.nn.sigmoid(_dot(h, wga_ref[...])).astype(BF16)
    gb_ref[0] = jax.nn.sigmoid(_dot(h, wgb_ref[...])).astype(BF16)


def _attn_constants():
    pe = [[0.0] * ATTN_WIDTH for _ in range(LANES)]
    po = [[0.0] * ATTN_WIDTH for _ in range(LANES)]
    ones = [[0.0] * ATTN_WIDTH for _ in range(SUBLANES)]
    for pair in range(NUM_HEADS // 2):
        for i in range(BIAS_PIECES):
            pe[2 * pair + NUM_HEADS * i][LANES * pair + HEAD_DIM + i] = 1.0
            po[2 * pair + 1 + NUM_HEADS * i][LANES * pair + i] = 1.0
            ones[0][LANES * pair + HEAD_DIM + i] = 1.0
            ones[1][LANES * pair + i] = 1.0
    return jnp.array(pe, BF16), jnp.array(po, BF16), jnp.array(ones, F32)


def _inproj(x, layer, g, wq, wk, wv, wf, bf, wu, wga, wgb):
    b, s, _ = x.shape
    tm = min(TM_PROJ, s)
    pe, po, ones = _attn_constants()
    tok = lambda width: pl.BlockSpec((1, tm, width), lambda i, j: (i, j, 0))
    act = lambda width: jax.ShapeDtypeStruct((b, s, width), BF16)
    consts = (g, wq, wk, wv, wf, bf, pe, po, ones, wu, wga, wgb)
    shared = (pe, po, ones)
    return pl.pallas_call(
        _inproj_kernel,
        grid=(b, s // tm),
        in_specs=[tok(D_MODEL)] + [
            _const_spec(c.shape, None if any(c is t for t in shared) else layer) for c in consts],
        out_specs=[tok(ATTN_WIDTH)] * 6 + [tok(SSM_WIDTH), tok(D_MODEL), tok(D_MODEL)],
        out_shape=[act(ATTN_WIDTH)] * 6 + [act(SSM_WIDTH), act(D_MODEL), act(D_MODEL)],
        scratch_shapes=[pltpu.VMEM((SUBLANES, LANES), F32)],
        compiler_params=pltpu.CompilerParams(
            dimension_semantics=("arbitrary", "arbitrary"), vmem_limit_bytes=VMEM_LIMIT_BYTES),
        name="inproj",
    )(x, *consts)


def _attn_kernel(qe_ref, qo_ref, ke_ref, ko_ref, ve_ref, vo_ref, tri_ref, o_ref, m_ref, acc_ref):
    i = pl.program_id(1)
    tq = o_ref.shape[1]

    def block(j, row0, bias, init):
        keys = pl.ds(pl.multiple_of(j * TK, TK), TK)
        rows = slice(row0, tq)
        for head in range(NUM_HEADS):
            ls = slice(LANES * (head // 2), LANES * (head // 2 + 1))
            q_ref, k_ref, v_ref = (qe_ref, ke_ref, ve_ref) if head % 2 == 0 else (qo_ref, ko_ref, vo_ref)
            s = lax.dot_general(q_ref[0, rows, ls], k_ref[0, keys, ls], (((1,), (1,)), ((), ())),
                                preferred_element_type=F32)
            if bias is not None:
                s = s + bias
            s_cols = [s[:, LANES * c:LANES * (c + 1)] for c in range(TK // LANES)]
            m_blk = jnp.max(functools.reduce(jnp.maximum, s_cols), axis=-1, keepdims=True)
            if init:
                m_new = jnp.broadcast_to(m_blk, (tq - row0, LANES))
            else:
                m_old = m_ref[head, rows, :]
                m_new = jnp.maximum(m_old, m_blk)
            p = jnp.concatenate([jnp.exp2((sc - m_new).astype(BF16)) for sc in s_cols], axis=1)
            pv = _dot(p, v_ref[0, keys, ls])
            if init:
                acc_ref[head, rows, :] = pv
            else:
                acc_ref[head, rows, :] = jnp.exp2(m_old - m_new) * acc_ref[head, rows, :] + pv
            m_ref[head, rows, :] = m_new

    blocks_per_step = tq // TK
    for c in range(blocks_per_step):
        block(i * blocks_per_step + c, TK * c, tri_ref[TK * c:tq, TK * c:TK * (c + 1)], init=(c == 0))

    def body(jj, carry):
        for c in range(blocks_per_step):
            block(jj * blocks_per_step + c, 0, None, init=False)
        return carry

    lax.fori_loop(0, i, body, 0)

    low = _low_half((tq, LANES))
    for pair in range(NUM_HEADS // 2):
        even = acc_ref[2 * pair]
        odd = acc_ref[2 * pair + 1]
        den = pltpu.roll(jnp.where(low, odd, even), HEAD_DIM, axis=1)
        o_ref[0, :, LANES * pair:LANES * (pair + 1)] = (jnp.where(low, even, odd) / den).astype(BF16)


def _attention(qe, qo, ke, ko, ve, vo):
    b, s, _ = qe.shape
    tq = min(TQ, s)
    assert tq % TK == 0 and s % tq == 0
    tri = jnp.where(lax.broadcasted_iota(jnp.int32, (tq, tq), 0)
                    >= lax.broadcasted_iota(jnp.int32, (tq, tq), 1), 0.0, MASK_VALUE).astype(F32)
    q_spec = pl.BlockSpec((1, tq, ATTN_WIDTH), lambda i, j: (i, j, 0))
    kv_spec = pl.BlockSpec((1, s, ATTN_WIDTH), lambda i, j: (i, 0, 0))
    return pl.pallas_call(
        _attn_kernel,
        grid=(b, s // tq),
        in_specs=[q_spec, q_spec, kv_spec, kv_spec, kv_spec, kv_spec, _const_spec(tri.shape)],
        out_specs=q_spec,
        out_shape=jax.ShapeDtypeStruct((b, s, ATTN_WIDTH), BF16),
        scratch_shapes=[pltpu.VMEM((NUM_HEADS, tq, LANES), F32),
                        pltpu.VMEM((NUM_HEADS, tq, LANES), F32)],
        compiler_params=pltpu.CompilerParams(
            dimension_semantics=("arbitrary", "arbitrary"), vmem_limit_bytes=VMEM_LIMIT_BYTES),
        name="attention",
    )(qe, qo, ke, ko, ve, vo, tri)


def _ssm_kernel(u_ref, un_ref, bw_ref, ar_ref, ai_ref, cw_ref, d_ref, wglu_ref, bglu_ref, o_ref,
                st0_ref, st1_ref, ub0_ref, ub1_ref, pin0_ref, pin1_ref, pout_ref, xr_ref, xi_ref):
    nb, lt2, width = u_ref.shape
    lt = lt2 // 2
    half_w = width // 2
    half_s = N_STATES // 2
    n_slab = width // LANES

    def to_time_major(u, perm_ref):
        uf = u.astype(F32)
        for s in range(n_slab):
            for b in range(nb):
                perm_ref[s, b * SSM_PITCH:b * SSM_PITCH + lt, :] = uf[b, :, LANES * s:LANES * (s + 1)]
        groups = [jnp.concatenate([perm_ref[s, pl.ds(t, nb, stride=SSM_PITCH), :] for s in range(n_slab)],
                                  axis=1) for t in range(lt)]
        return jnp.concatenate(groups, axis=0).astype(BF16)

    def store_seq_major(y, t0):
        for t in range(lt):
            for s in range(n_slab):
                pout_ref[s, pl.ds(t, nb, stride=SSM_PITCH), :] = (
                    y[nb * t:nb * (t + 1), LANES * s:LANES * (s + 1)])
        for b in range(nb):
            o_ref[b, t0:t0 + lt, :] = jnp.concatenate(
                [pout_ref[s, b * SSM_PITCH:b * SSM_PITCH + lt, :] for s in range(n_slab)],
                axis=1).astype(BF16)

    def project_in(ub_ref, st_ref):
        for part in range(4):
            half = part % 2
            st_ref[:, half_s * part:half_s * (part + 1)] = _dot(
                ub_ref[:, half_w * half:half_w * (half + 1)], bw_ref[part])

    def scan(st_ref):
        for chunk in range(N_STATES // LANES // SCAN_SLABS):
            cols = [slice(LANES * (chunk * SCAN_SLABS + s), LANES * (chunk * SCAN_SLABS + s + 1))
                    for s in range(SCAN_SLABS)]
            icols = [slice(N_STATES + c.start, N_STATES + c.stop) for c in cols]
            ar = [jnp.broadcast_to(ar_ref[:, c], (nb, LANES)) for c in cols]
            ai = [jnp.broadcast_to(ai_ref[:, c], (nb, LANES)) for c in cols]
            xr = [xr_ref[:, c] for c in cols]
            xi = [xi_ref[:, c] for c in cols]
            for t in range(lt):
                rows = slice(nb * t, nb * (t + 1))
                for s in range(SCAN_SLABS):
                    r = ar[s] * xr[s] - ai[s] * xi[s] + st_ref[rows, cols[s]]
                    m = ar[s] * xi[s] + ai[s] * xr[s] + st_ref[rows, icols[s]]
                    st_ref[rows, cols[s]] = r
                    st_ref[rows, icols[s]] = m
                    xr[s], xi[s] = r, m
            for s, c in enumerate(cols):
                xr_ref[:, c] = xr[s]
                xi_ref[:, c] = xi[s]

    def project_out(st_ref, ub_ref):
        ys = []
        for half in range(2):
            x_re = st_ref[:, half_s * half:half_s * (half + 1)].astype(BF16)
            x_im = st_ref[:, half_s * (2 + half):half_s * (3 + half)].astype(BF16)
            ys.append(_dot(x_re, cw_ref[half]) + _dot(x_im, cw_ref[2 + half]))
        y = jnp.concatenate(ys, axis=1) + d_ref[...] * ub_ref[...].astype(F32)
        y = jax.nn.gelu(y, approximate=True)
        return y * jax.nn.sigmoid(_dot(y.astype(BF16), wglu_ref[...]) + bglu_ref[...])

    @pl.when(pl.program_id(1) == 0)
    def _():
        xr_ref[...] = jnp.zeros_like(xr_ref)
        xi_ref[...] = jnp.zeros_like(xi_ref)
        ub0_ref[...] = to_time_major(u_ref[:, 0:lt, :], pin0_ref)
        project_in(ub0_ref, st0_ref)

    ub1_ref[...] = to_time_major(u_ref[:, lt:lt2, :], pin1_ref)
    project_in(ub1_ref, st1_ref)
    scan(st0_ref)
    store_seq_major(project_out(st0_ref, ub0_ref), 0)
    ub0_ref[...] = to_time_major(un_ref[...], pin0_ref)
    project_in(ub0_ref, st0_ref)
    scan(st1_ref)
    store_seq_major(project_out(st1_ref, ub1_ref), lt)


def _ssm(u, layer, bw, ar, ai, cw, d, wglu, bglu):
    b, s, _ = u.shape
    assert b % SSM_BATCH == 0 and s % (2 * LT) == 0
    last_tile = s // LT - 1
    rows = SSM_BATCH * LT
    tok = pl.BlockSpec((SSM_BATCH, 2 * LT, SSM_WIDTH), lambda i, j: (i, j, 0))
    nxt = pl.BlockSpec((SSM_BATCH, LT, SSM_WIDTH), lambda i, j: (i, jnp.minimum(2 * j + 2, last_tile), 0))
    state_buf = pltpu.VMEM((rows, 2 * N_STATES), F32)
    u_buf = pltpu.VMEM((rows, SSM_WIDTH), BF16)
    perm_buf = pltpu.VMEM((SSM_WIDTH // LANES, SSM_BATCH * SSM_PITCH, LANES), F32)
    return pl.pallas_call(
        _ssm_kernel,
        grid=(b // SSM_BATCH, s // (2 * LT)),
        in_specs=[tok, nxt] + [_const_spec(c.shape, layer) for c in (bw, ar, ai, cw, d, wglu, bglu)],
        out_specs=tok,
        out_shape=jax.ShapeDtypeStruct(u.shape, BF16),
        scratch_shapes=[state_buf, state_buf, u_buf, u_buf, perm_buf, perm_buf, perm_buf,
                        pltpu.VMEM((SSM_BATCH, N_STATES), F32),
                        pltpu.VMEM((SSM_BATCH, N_STATES), F32)],
        compiler_params=pltpu.CompilerParams(
            dimension_semantics=("arbitrary", "arbitrary"), vmem_limit_bytes=VMEM_LIMIT_BYTES),
        name="s5",
    )(u, u, bw, ar, ai, cw, d, wglu, bglu)


def _merge_mlp_kernel(x_ref, ya_ref, yb_ref, ga_ref, gb_ref, wa_ref, wb_ref, wo_ref, g_ref,
                      wup_ref, wdn_ref, gfin_ref, o_ref, *, final_norm):
    mixed = (ga_ref[...].astype(F32) * _dot(ya_ref[...], wa_ref[...])
             + gb_ref[...].astype(F32) * _dot(yb_ref[...], wb_ref[...]))
    x = x_ref[...] + _dot(mixed.astype(BF16), wo_ref[...])
    h = _rmsnorm(x, g_ref[...]).astype(BF16)
    for c in range(D_FF // FF_CHUNK):
        cs = slice(c * FF_CHUNK, (c + 1) * FF_CHUNK)
        up = jnp.maximum(_dot(h, wup_ref[:, cs]), 0.0)
        x = x + _dot((up * up).astype(BF16), wdn_ref[cs, :])
    if final_norm:
        x = _rmsnorm(x, gfin_ref[...])
    o_ref[...] = x


def _merge_mlp(x, ya, yb, ga, gb, layer, wa, wb, wo, g, wup, wdn, gfin, final_norm):
    t = x.shape[0]
    tm = min(TM_MLP, t)
    tok = lambda width: pl.BlockSpec((tm, width), lambda i: (i, 0))
    return pl.pallas_call(
        functools.partial(_merge_mlp_kernel, final_norm=final_norm),
        grid=(t // tm,),
        in_specs=[tok(D_MODEL), tok(ATTN_WIDTH), tok(SSM_WIDTH), tok(D_MODEL), tok(D_MODEL),
                  _const_spec(wa.shape, layer), _const_spec(wb.shape, layer),
                  _const_spec(wo.shape, layer), _const_spec(g.shape, layer),
                  _const_spec(wup.shape, layer), _const_spec(wdn.shape, layer),
                  _const_spec(gfin.shape)],
        out_specs=tok(D_MODEL),
        out_shape=jax.ShapeDtypeStruct(x.shape, F32),
        compiler_params=pltpu.CompilerParams(
            dimension_semantics=("arbitrary",), vmem_limit_bytes=VMEM_LIMIT_BYTES),
        name="merge_mlp",
    )(x, ya, yb, ga, gb, wa, wb, wo, g, wup, wdn, gfin)


def _ssm_params(lam_re, lam_im, log_dt, b_re, b_im, c_re, c_im):
    depth = lam_re.shape[0]
    dt = jnp.exp(log_dt)[..., None]
    mag = jnp.exp(lam_re * dt)
    ar = mag * jnp.cos(lam_im * dt)
    ai = mag * jnp.sin(lam_im * dt)
    den = lam_re * lam_re + lam_im * lam_im
    zr = ((ar - 1.0) * lam_re + ai * lam_im) / den
    zi = (ai * lam_re - (ar - 1.0) * lam_im) / den
    bb_re = zr[..., None] * b_re - zi[..., None] * b_im
    bb_im = zr[..., None] * b_im + zi[..., None] * b_re

    hg = SSM_GROUPS // 2
    eye = jnp.eye(hg, dtype=F32)

    def b_block(m):
        return jnp.einsum("lgpc,gh->lgchp", m, eye).reshape(depth, hg * SSM_GROUP_CH, hg * SSM_STATE)

    def c_block(m):
        return jnp.einsum("lgcp,gh->lgphc", m, eye).reshape(depth, hg * SSM_STATE, hg * SSM_GROUP_CH)

    bw = jnp.stack([b_block(bb_re[:, :hg]), b_block(bb_re[:, hg:]),
                    b_block(bb_im[:, :hg]), b_block(bb_im[:, hg:])], axis=1)
    cw = jnp.stack([c_block(c_re[:, :hg]), c_block(c_re[:, hg:]),
                    c_block(-c_im[:, :hg]), c_block(-c_im[:, hg:])], axis=1)
    return (bw.astype(BF16), ar.reshape(depth, 1, N_STATES), ai.reshape(depth, 1, N_STATES),
            cw.astype(BF16))


def _split_w_in(w_in, b_forget):
    depth = w_in.shape[0]
    o1 = ATTN_WIDTH
    o2 = o1 + ATTN_WIDTH
    o3 = o2 + ATTN_WIDTH
    o4 = o3 + NUM_HEADS
    o5 = o4 + SSM_WIDTH
    o6 = o5 + D_MODEL
    wq = (w_in[:, :, :o1] * (LOG2E * HEAD_DIM ** -0.5)).astype(BF16)
    reps = BIAS_PIECES * NUM_HEADS
    wf = jnp.zeros((depth, D_MODEL, LANES), F32).at[:, :, :reps].set(
        jnp.tile(w_in[:, :, o3:o4], (1, 1, BIAS_PIECES))).astype(BF16)
    bf = jnp.zeros((depth, 1, LANES), F32).at[:, 0, :reps].set(jnp.tile(b_forget, (1, BIAS_PIECES)))
    return (wq, w_in[:, :, o1:o2].astype(BF16), w_in[:, :, o2:o3].astype(BF16), wf, bf,
            w_in[:, :, o4:o5].astype(BF16), w_in[:, :, o5:o6].astype(BF16),
            w_in[:, :, o6:].astype(BF16))


@jax.jit
def kernel(x, norm_mix, w_in, b_forget, ssm_lambda_re, ssm_lambda_im, ssm_log_dt, ssm_b_re, ssm_b_im,
           ssm_c_re, ssm_c_im, ssm_d, w_glu, b_glu, w_branch_a, w_branch_b, w_out, norm_mlp,
           w_mlp_up, w_mlp_down, norm_final):
    b, s, d = x.shape
    depth = w_in.shape[0]
    t = b * s
    proj = (norm_mix.reshape(depth, 1, d),) + _split_w_in(w_in, b_forget)
    ssm = _ssm_params(ssm_lambda_re, ssm_lambda_im, ssm_log_dt, ssm_b_re, ssm_b_im, ssm_c_re, ssm_c_im)
    ssm = ssm + (ssm_d.reshape(depth, 1, SSM_WIDTH), w_glu.astype(BF16), b_glu.reshape(depth, 1, SSM_WIDTH))
    mlp = (w_branch_a.astype(BF16), w_branch_b.astype(BF16), w_out.astype(BF16),
           norm_mlp.reshape(depth, 1, d), w_mlp_up.astype(BF16), w_mlp_down.astype(BF16))
    gfin = norm_final.reshape(1, d)
    for l in range(depth):
        qe, qo, ke, ko, ve, vo, u, ga, gb = _inproj(x, l, *proj)
        ya = _attention(qe, qo, ke, ko, ve, vo)
        yb = _ssm(u, l, *ssm)
        x = _merge_mlp(x.reshape(t, d), ya.reshape(t, ATTN_WIDTH), yb.reshape(t, SSM_WIDTH),
                       ga.reshape(t, d), gb.reshape(t, d), l, *mlp, gfin,
                       final_norm=(l == depth - 1)).reshape(b, s, d)
    return x
```

```python
import functools
import math

import jax
import jax.numpy as jnp
from jax import lax
from jax.experimental import pallas as pl
from jax.experimental.pallas import tpu as pltpu

F32 = jnp.float32
BF16 = jnp.bfloat16

D_MODEL = 1024
NUM_HEADS = 8
HEAD_DIM = 64
ATTN_WIDTH = NUM_HEADS * HEAD_DIM
SSM_GROUPS = 32
SSM_GROUP_CH = 16
SSM_STATE = 64
SSM_WIDTH = SSM_GROUPS * SSM_GROUP_CH
N_STATES = SSM_GROUPS * SSM_STATE
D_FF = 4 * D_MODEL
RMS_EPS = 1e-6
MASK_VALUE = -1e30
LOG2E = math.log2(math.e)
BIAS_PIECES = 3

LANES = 128
SUBLANES = 8
VMEM_LIMIT_BYTES = 56 * 1024 * 1024

TM_PROJ = 512
TQ = 512
TK = 256
LT = 128
SSM_BATCH = SUBLANES
SSM_PITCH = LT + SUBLANES
SCAN_SLABS = 8
TM_MLP = 512
FF_CHUNK = 1024

_dot = functools.partial(jnp.dot, preferred_element_type=F32)


def _rmsnorm(x, g):
    ms = jnp.mean(x * x, axis=-1, keepdims=True)
    return x * lax.rsqrt(ms + RMS_EPS) * g


def _const_spec(shape, layer=None):
    if layer is None:
        zeros = (0,) * len(shape)
        return pl.BlockSpec(shape, lambda *_: zeros, pipeline_mode=pl.Buffered(1))
    index = (layer,) + (0,) * (len(shape) - 1)
    return pl.BlockSpec((None,) + tuple(shape[1:]), lambda *_: index, pipeline_mode=pl.Buffered(1))


def _low_half(shape):
    lane = lax.broadcasted_iota(jnp.int32, shape, len(shape) - 1)
    return (lane & (LANES - 1)) < HEAD_DIM


def _cumsum_rows(x):
    n = x.shape[0]
    row = lax.broadcasted_iota(jnp.int32, x.shape, 0)
    k = 1
    while k < n:
        x = x + jnp.where(row >= k, pltpu.roll(x, k, axis=0), 0.0)
        k *= 2
    return x


def _inproj_kernel(x_ref, g_ref, wq_ref, wk_ref, wv_ref, wf_ref, bf_ref, pe_ref, po_ref, one_ref,
                   wu_ref, wga_ref, wgb_ref,
                   qe_ref, qo_ref, ke_ref, ko_ref, ve_ref, vo_ref, u_ref, ga_ref, gb_ref, carry_ref):
    @pl.when(pl.program_id(1) == 0)
    def _():
        carry_ref[...] = jnp.zeros_like(carry_ref)

    tm = x_ref.shape[1]
    h = _rmsnorm(x_ref[0], g_ref[...]).astype(BF16)
    low = _low_half((tm, ATTN_WIDTH))
    q_ones = one_ref[0:1, :]
    q_ones_odd = one_ref[1:2, :]

    q = _dot(h, wq_ref[...])
    qe_ref[0] = jnp.where(low, q, q_ones).astype(BF16)
    qo_ref[0] = jnp.where(low, q_ones_odd, q).astype(BF16)
    v = _dot(h, wv_ref[...])
    ve_ref[0] = jnp.where(low, v, 1.0).astype(BF16)
    vo_ref[0] = jnp.where(low, 1.0, v).astype(BF16)

    fl = _dot(h, wf_ref[...]) + bf_ref[...]
    log_f = jnp.minimum(fl, 0.0) - jnp.log1p(jnp.exp(-jnp.abs(fl)))
    cum = _cumsum_rows(log_f) + carry_ref[0:1, :]
    carry_ref[...] = jnp.broadcast_to(cum[tm - 1:tm, :], carry_ref.shape)
    bias = cum * (-LOG2E)
    hi = bias.astype(BF16).astype(F32)
    mid = (bias - hi).astype(BF16).astype(F32)
    lo = (bias - hi - mid).astype(BF16).astype(F32)
    lane = lax.broadcasted_iota(jnp.int32, bias.shape, 1)
    pieces = jnp.where(lane < NUM_HEADS, hi, jnp.where(lane < 2 * NUM_HEADS, mid, lo)).astype(BF16)
    k = _dot(h, wk_ref[...])
    ke_ref[0] = jnp.where(low, k, _dot(pieces, pe_ref[...])).astype(BF16)
    ko_ref[0] = jnp.where(low, _dot(pieces, po_ref[...]), k).astype(BF16)

    u_ref[0] = _dot(h, wu_ref[...]).astype(BF16)
    ga_ref[0] = jax.nn.sigmoid(_dot(h, wga_ref[...])).astype(BF16)
    gb_ref[0] = jax.nn.sigmoid(_dot(h, wgb_ref[...])).astype(BF16)


def _attn_constants():
    pe = [[0.0] * ATTN_WIDTH for _ in range(LANES)]
    po = [[0.0] * ATTN_WIDTH for _ in range(LANES)]
    ones = [[0.0] * ATTN_WIDTH for _ in range(SUBLANES)]
    for pair in range(NUM_HEADS // 2):
        for i in range(BIAS_PIECES):
            pe[2 * pair + NUM_HEADS * i][LANES * pair + HEAD_DIM + i] = 1.0
            po[2 * pair + 1 + NUM_HEADS * i][LANES * pair + i] = 1.0
            ones[0][LANES * pair + HEAD_DIM + i] = 1.0
            ones[1][LANES * pair + i] = 1.0
    return jnp.array(pe, BF16), jnp.array(po, BF16), jnp.array(ones, F32)


def _inproj(x, layer, g, wq, wk, wv, wf, bf, wu, wga, wgb):
    b, s, _ = x.shape
    tm = min(TM_PROJ, s)
    pe, po, ones = _attn_constants()
    tok = lambda width: pl.BlockSpec((1, tm, width), lambda i, j: (i, j, 0))
    act = lambda width: jax.ShapeDtypeStruct((b, s, width), BF16)
    consts = (g, wq, wk, wv, wf, bf, pe, po, ones, wu, wga, wgb)
    shared = (pe, po, ones)
    return pl.pallas_call(
        _inproj_kernel,
        grid=(b, s // tm),
        in_specs=[tok(D_MODEL)] + [
            _const_spec(c.shape, None if any(c is t for t in shared) else layer) for c in consts],
        out_specs=[tok(ATTN_WIDTH)] * 6 + [tok(SSM_WIDTH), tok(D_MODEL), tok(D_MODEL)],
        out_shape=[act(ATTN_WIDTH)] * 6 + [act(SSM_WIDTH), act(D_MODEL), act(D_MODEL)],
        scratch_shapes=[pltpu.VMEM((SUBLANES, LANES), F32)],
        compiler_params=pltpu.CompilerParams(
            dimension_semantics=("arbitrary", "arbitrary"), vmem_limit_bytes=VMEM_LIMIT_BYTES),
        name="inproj",
    )(x, *consts)


def _attn_kernel(qe_ref, qo_ref, ke_ref, ko_ref, ve_ref, vo_ref, tri_ref, o_ref, m_ref, acc_ref):
    i = pl.program_id(1)
    tq = o_ref.shape[1]

    def block(j, row0, bias, init):
        keys = pl.ds(pl.multiple_of(j * TK, TK), TK)
        rows = slice(row0, tq)
        for head in range(NUM_HEADS):
            ls = slice(LANES * (head // 2), LANES * (head // 2 + 1))
            q_ref, k_ref, v_ref = (qe_ref, ke_ref, ve_ref) if head % 2 == 0 else (qo_ref, ko_ref, vo_ref)
            s = lax.dot_general(q_ref[0, rows, ls], k_ref[0, keys, ls], (((1,), (1,)), ((), ())),
                                preferred_element_type=F32)
            if bias is not None:
                s = s + bias
            s_cols = [s[:, LANES * c:LANES * (c + 1)] for c in range(TK // LANES)]
            m_blk = jnp.max(functools.reduce(jnp.maximum, s_cols), axis=-1, keepdims=True)
            if init:
                m_new = jnp.broadcast_to(m_blk, (tq - row0, LANES))
            else:
                m_old = m_ref[head, rows, :]
                m_new = jnp.maximum(m_old, m_blk)
            p = jnp.concatenate([jnp.exp2((sc - m_new).astype(BF16)) for sc in s_cols], axis=1)
            pv = _dot(p, v_ref[0, keys, ls])
            if init:
                acc_ref[head, rows, :] = pv
            else:
                acc_ref[head, rows, :] = jnp.exp2(m_old - m_new) * acc_ref[head, rows, :] + pv
            m_ref[head, rows, :] = m_new

    blocks_per_step = tq // TK
    for c in range(blocks_per_step):
        block(i * blocks_per_step + c, TK * c, tri_ref[TK * c:tq, TK * c:TK * (c + 1)], init=(c == 0))

    def body(jj, carry):
        for c in range(blocks_per_step):
            block(jj * blocks_per_step + c, 0, None, init=False)
        return carry

    lax.fori_loop(0, i, body, 0)

    low = _low_half((tq, LANES))
    for pair in range(NUM_HEADS // 2):
        even = acc_ref[2 * pair]
        odd = acc_ref[2 * pair + 1]
        den = pltpu.roll(jnp.where(low, odd, even), HEAD_DIM, axis=1)
        o_ref[0, :, LANES * pair:LANES * (pair + 1)] = (jnp.where(low, even, odd) / den).astype(BF16)


def _attention(qe, qo, ke, ko, ve, vo):
    b, s, _ = qe.shape
    tq = min(TQ, s)
    assert tq % TK == 0 and s % tq == 0
    tri = jnp.where(lax.broadcasted_iota(jnp.int32, (tq, tq), 0)
                    >= lax.broadcasted_iota(jnp.int32, (tq, tq), 1), 0.0, MASK_VALUE).astype(F32)
    q_spec = pl.BlockSpec((1, tq, ATTN_WIDTH), lambda i, j: (i, j, 0))
    kv_spec = pl.BlockSpec((1, s, ATTN_WIDTH), lambda i, j: (i, 0, 0))
    return pl.pallas_call(
        _attn_kernel,
        grid=(b, s // tq),
        in_specs=[q_spec, q_spec, kv_spec, kv_spec, kv_spec, kv_spec, _const_spec(tri.shape)],
        out_specs=q_spec,
        out_shape=jax.ShapeDtypeStruct((b, s, ATTN_WIDTH), BF16),
        scratch_shapes=[pltpu.VMEM((NUM_HEADS, tq, LANES), F32),
                        pltpu.VMEM((NUM_HEADS, tq, LANES), F32)],
        compiler_params=pltpu.CompilerParams(
            dimension_semantics=("arbitrary", "arbitrary"), vmem_limit_bytes=VMEM_LIMIT_BYTES),
        name="attention",
    )(qe, qo, ke, ko, ve, vo, tri)


def _ssm_kernel(u_ref, un_ref, bw_ref, ar_ref, ai_ref, cw_ref, cb_ref, d_ref, wglu_ref, bglu_ref, o_ref,
                st0_ref, st1_ref, ub0_ref, ub1_ref, pin0_ref, pin1_ref, pout_ref, x_ref):
    nb, lt2, width = u_ref.shape
    lt = lt2 // 2
    npair = lt // 2
    prow = nb * npair
    n_set = width // LANES
    set_w = 2 * N_STATES // n_set
    half_set = set_w // 2

    def to_pairs(u, perm_ref):
        uf = u.astype(F32)
        for s in range(n_set):
            for b in range(nb):
                perm_ref[s, b * SSM_PITCH:b * SSM_PITCH + lt, :] = uf[b, :, LANES * s:LANES * (s + 1)]
        step = lambda t, s: perm_ref[s, pl.ds(t, nb, stride=SSM_PITCH), :]
        rows = [jnp.concatenate([step(2 * k + par, s) for s in range(n_set) for par in range(2)], axis=1)
                for k in range(npair)]
        return jnp.concatenate(rows, axis=0).astype(BF16)

    def store_seq_major(y, t0):
        for k in range(npair):
            for par in range(2):
                src = y[par * prow + nb * k:par * prow + nb * (k + 1)]
                for s in range(n_set):
                    pout_ref[s, pl.ds(2 * k + par, nb, stride=SSM_PITCH), :] = src[:, LANES * s:LANES * (s + 1)]
        for b in range(nb):
            o_ref[b, t0:t0 + lt, :] = jnp.concatenate(
                [pout_ref[s, b * SSM_PITCH:b * SSM_PITCH + lt, :] for s in range(n_set)],
                axis=1).astype(BF16)

    def project_in(ub_ref, st_ref):
        for j in range(n_set):
            st_ref[nb:nb + prow, set_w * j:set_w * (j + 1)] = _dot(
                ub_ref[:, 2 * LANES * j:2 * LANES * (j + 1)], bw_ref[j])

    def scan(st_ref):
        st_ref[0:nb, :] = x_ref[...]
        vregs_per_set = half_set // LANES
        for chunk in range(n_set * vregs_per_set // SCAN_SLABS):
            cols = []
            for v in range(chunk * SCAN_SLABS, (chunk + 1) * SCAN_SLABS):
                j, q = divmod(v, vregs_per_set)
                cols.append((set_w * j + LANES * q, set_w * j + half_set + LANES * q, LANES * v))
            ar = [jnp.broadcast_to(ar_ref[:, n:n + LANES], (nb, LANES)) for _, _, n in cols]
            ai = [jnp.broadcast_to(ai_ref[:, n:n + LANES], (nb, LANES)) for _, _, n in cols]
            xr = [st_ref[0:nb, rc:rc + LANES] for rc, _, _ in cols]
            xi = [st_ref[0:nb, ic:ic + LANES] for _, ic, _ in cols]
            for k in range(npair):
                rows = slice(nb * (k + 1), nb * (k + 2))
                for s, (rc, ic, _) in enumerate(cols):
                    r = ar[s] * xr[s] - ai[s] * xi[s] + st_ref[rows, rc:rc + LANES]
                    m = ar[s] * xi[s] + ai[s] * xr[s] + st_ref[rows, ic:ic + LANES]
                    st_ref[rows, rc:rc + LANES] = r
                    st_ref[rows, ic:ic + LANES] = m
                    xr[s], xi[s] = r, m
        x_ref[...] = st_ref[prow:prow + nb, :]

    def project_out(st_ref, ub_ref):
        y2 = [_dot(st_ref[:, set_w * j:set_w * (j + 1)].astype(BF16), cw_ref[j]) for j in range(n_set)]
        y_odd = jnp.concatenate([y[nb:, 0:LANES] for y in y2], axis=1)
        from_state = jnp.concatenate([y[:prow, LANES:2 * LANES] for y in y2], axis=1)
        u_even = jnp.concatenate(
            [ub_ref[:, 2 * LANES * s:2 * LANES * s + LANES] for s in range(n_set)], axis=1)
        u_odd = jnp.concatenate(
            [ub_ref[:, 2 * LANES * s + LANES:2 * LANES * (s + 1)] for s in range(n_set)], axis=1)
        half_w = width // 2
        direct = jnp.concatenate(
            [_dot(u_even[:, half_w * h:half_w * (h + 1)], cb_ref[h]) for h in range(2)], axis=1)
        y = jnp.concatenate([from_state + direct + d_ref[...] * u_even.astype(F32),
                             y_odd + d_ref[...] * u_odd.astype(F32)], axis=0)
        y = jax.nn.gelu(y, approximate=True)
        return y * jax.nn.sigmoid(_dot(y.astype(BF16), wglu_ref[...]) + bglu_ref[...])

    @pl.when(pl.program_id(1) == 0)
    def _():
        x_ref[...] = jnp.zeros_like(x_ref)
        ub0_ref[...] = to_pairs(u_ref[:, 0:lt, :], pin0_ref)
        project_in(ub0_ref, st0_ref)

    ub1_ref[...] = to_pairs(u_ref[:, lt:lt2, :], pin1_ref)
    project_in(ub1_ref, st1_ref)
    scan(st0_ref)
    store_seq_major(project_out(st0_ref, ub0_ref), 0)
    ub0_ref[...] = to_pairs(un_ref[...], pin0_ref)
    project_in(ub0_ref, st0_ref)
    scan(st1_ref)
    store_seq_major(project_out(st1_ref, ub1_ref), lt)


def _ssm(u, layer, bw, ar, ai, cw, cb, d, wglu, bglu):
    b, s, _ = u.shape
    assert b % SSM_BATCH == 0 and s % (2 * LT) == 0 and LT % 2 == 0
    last_tile = s // LT - 1
    prow = SSM_BATCH * LT // 2
    tok = pl.BlockSpec((SSM_BATCH, 2 * LT, SSM_WIDTH), lambda i, j: (i, j, 0))
    nxt = pl.BlockSpec((SSM_BATCH, LT, SSM_WIDTH), lambda i, j: (i, jnp.minimum(2 * j + 2, last_tile), 0))
    state_buf = pltpu.VMEM((SSM_BATCH + prow, 2 * N_STATES), F32)
    u_buf = pltpu.VMEM((prow, 2 * SSM_WIDTH), BF16)
    perm_buf = pltpu.VMEM((SSM_WIDTH // LANES, SSM_BATCH * SSM_PITCH, LANES), F32)
    consts = (bw, ar, ai, cw, cb, d, wglu, bglu)
    return pl.pallas_call(
        _ssm_kernel,
        grid=(b // SSM_BATCH, s // (2 * LT)),
        in_specs=[tok, nxt] + [_const_spec(c.shape, layer) for c in consts],
        out_specs=tok,
        out_shape=jax.ShapeDtypeStruct(u.shape, BF16),
        scratch_shapes=[state_buf, state_buf, u_buf, u_buf, perm_buf, perm_buf, perm_buf,
                        pltpu.VMEM((SSM_BATCH, 2 * N_STATES), F32)],
        compiler_params=pltpu.CompilerParams(
            dimension_semantics=("arbitrary", "arbitrary"), vmem_limit_bytes=VMEM_LIMIT_BYTES),
        name="s5",
    )(u, u, *consts)


def _merge_mlp_kernel(x_ref, ya_ref, yb_ref, ga_ref, gb_ref, wa_ref, wb_ref, wo_ref, g_ref,
                      wup_ref, wdn_ref, gfin_ref, o_ref, *, final_norm):
    mixed = (ga_ref[...].astype(F32) * _dot(ya_ref[...], wa_ref[...])
             + gb_ref[...].astype(F32) * _dot(yb_ref[...], wb_ref[...]))
    x = x_ref[...] + _dot(mixed.astype(BF16), wo_ref[...])
    h = _rmsnorm(x, g_ref[...]).astype(BF16)
    for c in range(D_FF // FF_CHUNK):
        cs = slice(c * FF_CHUNK, (c + 1) * FF_CHUNK)
        up = jnp.maximum(_dot(h, wup_ref[:, cs]), 0.0)
        x = x + _dot((up * up).astype(BF16), wdn_ref[cs, :])
    if final_norm:
        x = _rmsnorm(x, gfin_ref[...])
    o_ref[...] = x


def _merge_mlp(x, ya, yb, ga, gb, layer, wa, wb, wo, g, wup, wdn, gfin, final_norm):
    t = x.shape[0]
    tm = min(TM_MLP, t)
    tok = lambda width: pl.BlockSpec((tm, width), lambda i: (i, 0))
    return pl.pallas_call(
        functools.partial(_merge_mlp_kernel, final_norm=final_norm),
        grid=(t // tm,),
        in_specs=[tok(D_MODEL), tok(ATTN_WIDTH), tok(SSM_WIDTH), tok(D_MODEL), tok(D_MODEL),
                  _const_spec(wa.shape, layer), _const_spec(wb.shape, layer),
                  _const_spec(wo.shape, layer), _const_spec(g.shape, layer),
                  _const_spec(wup.shape, layer), _const_spec(wdn.shape, layer),
                  _const_spec(gfin.shape)],
        out_specs=tok(D_MODEL),
        out_shape=jax.ShapeDtypeStruct(x.shape, F32),
        compiler_params=pltpu.CompilerParams(
            dimension_semantics=("arbitrary",), vmem_limit_bytes=VMEM_LIMIT_BYTES),
        name="merge_mlp",
    )(x, ya, yb, ga, gb, wa, wb, wo, g, wup, wdn, gfin)


def _ssm_params(lam_re, lam_im, log_dt, b_re, b_im, c_re, c_im):
    depth = lam_re.shape[0]
    dt = jnp.exp(log_dt)[..., None]
    mag = jnp.exp(lam_re * dt)
    ar = mag * jnp.cos(lam_im * dt)
    ai = mag * jnp.sin(lam_im * dt)
    den = lam_re * lam_re + lam_im * lam_im
    zr = ((ar - 1.0) * lam_re + ai * lam_im) / den
    zi = (ai * lam_re - (ar - 1.0) * lam_im) / den
    bb_re = zr[..., None] * b_re - zi[..., None] * b_im
    bb_im = zr[..., None] * b_im + zi[..., None] * b_re

    a2r = ar * ar - ai * ai
    a2i = 2.0 * ar * ai
    abb_re = ar[..., None] * bb_re - ai[..., None] * bb_im
    abb_im = ar[..., None] * bb_im + ai[..., None] * bb_re
    ca_re = c_re * ar[:, :, None, :] - c_im * ai[:, :, None, :]
    ca_im = c_re * ai[:, :, None, :] + c_im * ar[:, :, None, :]
    cb = (jnp.einsum("lgop,lgpi->lgio", c_re, bb_re) - jnp.einsum("lgop,lgpi->lgio", c_im, bb_im))

    n_set = SSM_WIDTH // LANES
    gs = SSM_GROUPS // n_set
    eye = jnp.eye(gs, dtype=F32)

    def b_block(m):
        return jnp.einsum("lgpc,gh->lgchp", m, eye).reshape(depth, gs * SSM_GROUP_CH, gs * SSM_STATE)

    def c_block(m):
        return jnp.einsum("lgcp,gh->lgphc", m, eye).reshape(depth, gs * SSM_STATE, gs * SSM_GROUP_CH)

    bw, cw = [], []
    for j in range(n_set):
        g = slice(gs * j, gs * (j + 1))
        bw.append(jnp.concatenate(
            [jnp.concatenate([b_block(abb_re[:, g]), b_block(abb_im[:, g])], axis=2),
             jnp.concatenate([b_block(bb_re[:, g]), b_block(bb_im[:, g])], axis=2)], axis=1))
        cw.append(jnp.concatenate(
            [jnp.concatenate([c_block(c_re[:, g]), c_block(ca_re[:, g])], axis=2),
             jnp.concatenate([c_block(-c_im[:, g]), c_block(-ca_im[:, g])], axis=2)], axis=1))
    hg = SSM_GROUPS // 2
    eye_h = jnp.eye(hg, dtype=F32)
    cbw = jnp.stack([jnp.einsum("lgio,gh->lgiho", cb[:, hg * h:hg * (h + 1)], eye_h).reshape(
        depth, hg * SSM_GROUP_CH, hg * SSM_GROUP_CH) for h in range(2)], axis=1)
    return (jnp.stack(bw, axis=1).astype(BF16), a2r.reshape(depth, 1, N_STATES),
            a2i.reshape(depth, 1, N_STATES), jnp.stack(cw, axis=1).astype(BF16), cbw.astype(BF16))


def _split_w_in(w_in, b_forget):
    depth = w_in.shape[0]
    o1 = ATTN_WIDTH
    o2 = o1 + ATTN_WIDTH
    o3 = o2 + ATTN_WIDTH
    o4 = o3 + NUM_HEADS
    o5 = o4 + SSM_WIDTH
    o6 = o5 + D_MODEL
    wq = (w_in[:, :, :o1] * (LOG2E * HEAD_DIM ** -0.5)).astype(BF16)
    reps = BIAS_PIECES * NUM_HEADS
    wf = jnp.zeros((depth, D_MODEL, LANES), F32).at[:, :, :reps].set(
        jnp.tile(w_in[:, :, o3:o4], (1, 1, BIAS_PIECES))).astype(BF16)
    bf = jnp.zeros((depth, 1, LANES), F32).at[:, 0, :reps].set(jnp.tile(b_forget, (1, BIAS_PIECES)))
    return (wq, w_in[:, :, o1:o2].astype(BF16), w_in[:, :, o2:o3].astype(BF16), wf, bf,
            w_in[:, :, o4:o5].astype(BF16), w_in[:, :, o5:o6].astype(BF16),
            w_in[:, :, o6:].astype(BF16))


@jax.jit
def kernel(x, norm_mix, w_in, b_forget, ssm_lambda_re, ssm_lambda_im, ssm_log_dt, ssm_b_re, ssm_b_im,
           ssm_c_re, ssm_c_im, ssm_d, w_glu, b_glu, w_branch_a, w_branch_b, w_out, norm_mlp,
           w_mlp_up, w_mlp_down, norm_final):
    b, s, d = x.shape
    depth = w_in.shape[0]
    t = b * s
    proj = (norm_mix.reshape(depth, 1, d),) + _split_w_in(w_in, b_forget)
    ssm = _ssm_params(ssm_lambda_re, ssm_lambda_im, ssm_log_dt, ssm_b_re, ssm_b_im, ssm_c_re, ssm_c_im)
    ssm = ssm + (ssm_d.reshape(depth, 1, SSM_WIDTH), w_glu.astype(BF16), b_glu.reshape(depth, 1, SSM_WIDTH))
    mlp = (w_branch_a.astype(BF16), w_branch_b.astype(BF16), w_out.astype(BF16),
           norm_mlp.reshape(depth, 1, d), w_mlp_up.astype(BF16), w_mlp_down.astype(BF16))
    gfin = norm_final.reshape(1, d)
    for l in range(depth):
        qe, qo, ke, ko, ve, vo, u, ga, gb = _inproj(x, l, *proj)
        ya = _attention(qe, qo, ke, ko, ve, vo)
        yb = _ssm(u, l, *ssm)
        x = _merge_mlp(x.reshape(t, d), ya.reshape(t, ATTN_WIDTH), yb.reshape(t, SSM_WIDTH),
                       ga.reshape(t, d), gb.reshape(t, d), l, *mlp, gfin,
                       final_norm=(l == depth - 1)).reshape(b, s, d)
    return x
```

```python
import functools
import math

import jax
import jax.numpy as jnp
from jax import lax
from jax.experimental import pallas as pl
from jax.experimental.pallas import tpu as pltpu

F32 = jnp.float32
BF16 = jnp.bfloat16

D_MODEL = 1024
NUM_HEADS = 8
HEAD_DIM = 64
ATTN_WIDTH = NUM_HEADS * HEAD_DIM
SSM_GROUPS = 32
SSM_GROUP_CH = 16
SSM_STATE = 64
SSM_WIDTH = SSM_GROUPS * SSM_GROUP_CH
N_STATES = SSM_GROUPS * SSM_STATE
D_FF = 4 * D_MODEL
RMS_EPS = 1e-6
MASK_VALUE = -1e30
LOG2E = math.log2(math.e)
BIAS_PIECES = 3

LANES = 128
SUBLANES = 8
VMEM_LIMIT_BYTES = 56 * 1024 * 1024

TM_PROJ = 1024
TQ = 1024
TK = 256
LT = 128
SSM_BATCH = SUBLANES
SSM_PITCH = LT + SUBLANES
SCAN_SLABS = 8
TM_MLP = 512
FF_CHUNK = 1024

_dot = functools.partial(jnp.dot, preferred_element_type=F32)


def _rmsnorm(x, g):
    ms = jnp.mean(x * x, axis=-1, keepdims=True)
    return x * lax.rsqrt(ms + RMS_EPS) * g


def _const_spec(shape, layer=None):
    if layer is None:
        zeros = (0,) * len(shape)
        return pl.BlockSpec(shape, lambda *_: zeros, pipeline_mode=pl.Buffered(1))
    index = (layer,) + (0,) * (len(shape) - 1)
    return pl.BlockSpec((None,) + tuple(shape[1:]), lambda *_: index, pipeline_mode=pl.Buffered(1))


def _low_half(shape):
    lane = lax.broadcasted_iota(jnp.int32, shape, len(shape) - 1)
    return (lane & (LANES - 1)) < HEAD_DIM


def _cumsum_rows(x):
    n = x.shape[0]
    row = lax.broadcasted_iota(jnp.int32, x.shape, 0)
    k = 1
    while k < n:
        x = x + jnp.where(row >= k, pltpu.roll(x, k, axis=0), 0.0)
        k *= 2
    return x


def _inproj_kernel(x_ref, g_ref, wq_ref, wk_ref, wv_ref, wf_ref, bf_ref, pe_ref, po_ref, one_ref,
                   wu_ref, wga_ref, wgb_ref,
                   qe_ref, qo_ref, ke_ref, ko_ref, ve_ref, vo_ref, u_ref, ga_ref, gb_ref, carry_ref):
    @pl.when(pl.program_id(1) == 0)
    def _():
        carry_ref[...] = jnp.zeros_like(carry_ref)

    tm = x_ref.shape[1]
    h = _rmsnorm(x_ref[0], g_ref[...]).astype(BF16)
    low = _low_half((tm, ATTN_WIDTH))
    q_ones = one_ref[0:1, :]
    q_ones_odd = one_ref[1:2, :]

    q = _dot(h, wq_ref[...])
    qe_ref[0] = jnp.where(low, q, q_ones).astype(BF16)
    qo_ref[0] = jnp.where(low, q_ones_odd, q).astype(BF16)
    v = _dot(h, wv_ref[...])
    ve_ref[0] = jnp.where(low, v, 1.0).astype(BF16)
    vo_ref[0] = jnp.where(low, 1.0, v).astype(BF16)

    fl = _dot(h, wf_ref[...]) + bf_ref[...]
    log_f = jnp.minimum(fl, 0.0) - jnp.log1p(jnp.exp(-jnp.abs(fl)))
    cum = _cumsum_rows(log_f) + carry_ref[0:1, :]
    carry_ref[...] = jnp.broadcast_to(cum[tm - 1:tm, :], carry_ref.shape)
    bias = cum * (-LOG2E)
    hi = bias.astype(BF16).astype(F32)
    mid = (bias - hi).astype(BF16).astype(F32)
    lo = (bias - hi - mid).astype(BF16).astype(F32)
    lane = lax.broadcasted_iota(jnp.int32, bias.shape, 1)
    pieces = jnp.where(lane < NUM_HEADS, hi, jnp.where(lane < 2 * NUM_HEADS, mid, lo)).astype(BF16)
    k = _dot(h, wk_ref[...])
    ke_ref[0] = jnp.where(low, k, _dot(pieces, pe_ref[...])).astype(BF16)
    ko_ref[0] = jnp.where(low, _dot(pieces, po_ref[...]), k).astype(BF16)

    u_ref[0] = _dot(h, wu_ref[...]).astype(BF16)
    ga_ref[0] = jax.nn.sigmoid(_dot(h, wga_ref[...])).astype(BF16)
    gb_ref[0] = jax.nn.sigmoid(_dot(h, wgb_ref[...])).astype(BF16)


def _attn_constants():
    pe = [[0.0] * ATTN_WIDTH for _ in range(LANES)]
    po = [[0.0] * ATTN_WIDTH for _ in range(LANES)]
    ones = [[0.0] * ATTN_WIDTH for _ in range(SUBLANES)]
    for pair in range(NUM_HEADS // 2):
        for i in range(BIAS_PIECES):
            pe[2 * pair + NUM_HEADS * i][LANES * pair + HEAD_DIM + i] = 1.0
            po[2 * pair + 1 + NUM_HEADS * i][LANES * pair + i] = 1.0
            ones[0][LANES * pair + HEAD_DIM + i] = 1.0
            ones[1][LANES * pair + i] = 1.0
    return jnp.array(pe, BF16), jnp.array(po, BF16), jnp.array(ones, F32)


def _inproj(x, layer, g, wq, wk, wv, wf, bf, wu, wga, wgb):
    b, s, _ = x.shape
    tm = min(TM_PROJ, s)
    pe, po, ones = _attn_constants()
    tok = lambda width: pl.BlockSpec((1, tm, width), lambda i, j: (i, j, 0))
    act = lambda width: jax.ShapeDtypeStruct((b, s, width), BF16)
    consts = (g, wq, wk, wv, wf, bf, pe, po, ones, wu, wga, wgb)
    shared = (pe, po, ones)
    return pl.pallas_call(
        _inproj_kernel,
        grid=(b, s // tm),
        in_specs=[tok(D_MODEL)] + [
            _const_spec(c.shape, None if any(c is t for t in shared) else layer) for c in consts],
        out_specs=[tok(ATTN_WIDTH)] * 6 + [tok(SSM_WIDTH), tok(D_MODEL), tok(D_MODEL)],
        out_shape=[act(ATTN_WIDTH)] * 6 + [act(SSM_WIDTH), act(D_MODEL), act(D_MODEL)],
        scratch_shapes=[pltpu.VMEM((SUBLANES, LANES), F32)],
        compiler_params=pltpu.CompilerParams(
            dimension_semantics=("arbitrary", "arbitrary"), vmem_limit_bytes=VMEM_LIMIT_BYTES),
        name="inproj",
    )(x, *consts)


def _attn_kernel(qe_ref, qo_ref, ke_ref, ko_ref, ve_ref, vo_ref, tri_ref, o_ref, m_ref, acc_ref):
    i = pl.program_id(1)
    tq = o_ref.shape[1]

    def block(j, row0, diagonal, init):
        keys = pl.ds(pl.multiple_of(j * TK, TK), TK)
        rows = slice(row0, tq)
        for head in range(NUM_HEADS):
            ls = slice(LANES * (head // 2), LANES * (head // 2 + 1))
            q_ref, k_ref, v_ref = (qe_ref, ke_ref, ve_ref) if head % 2 == 0 else (qo_ref, ko_ref, vo_ref)
            s = lax.dot_general(q_ref[0, rows, ls], k_ref[0, keys, ls], (((1,), (1,)), ((), ())),
                                preferred_element_type=F32)
            if diagonal:
                top = s[:TK] + tri_ref[...]
                s = top if tq - row0 == TK else jnp.concatenate([top, s[TK:]], axis=0)
            s_cols = [s[:, LANES * c:LANES * (c + 1)] for c in range(TK // LANES)]
            m_blk = jnp.max(functools.reduce(jnp.maximum, s_cols), axis=-1, keepdims=True)
            if init:
                m_new = jnp.broadcast_to(m_blk, (tq - row0, LANES))
            else:
                m_old = m_ref[head, rows, :]
                m_new = jnp.maximum(m_old, m_blk)
            p = jnp.concatenate([jnp.exp2((sc - m_new).astype(BF16)) for sc in s_cols], axis=1)
            pv = _dot(p, v_ref[0, keys, ls])
            if init:
                acc_ref[head, rows, :] = pv
            else:
                acc_ref[head, rows, :] = jnp.exp2(m_old - m_new) * acc_ref[head, rows, :] + pv
            m_ref[head, rows, :] = m_new

    blocks_per_step = tq // TK
    for c in range(blocks_per_step):
        block(i * blocks_per_step + c, TK * c, True, init=(c == 0))

    def body(jj, carry):
        for c in range(blocks_per_step):
            block(jj * blocks_per_step + c, 0, False, init=False)
        return carry

    lax.fori_loop(0, i, body, 0)

    low = _low_half((tq, LANES))
    for pair in range(NUM_HEADS // 2):
        even = acc_ref[2 * pair]
        odd = acc_ref[2 * pair + 1]
        den = pltpu.roll(jnp.where(low, odd, even), HEAD_DIM, axis=1)
        o_ref[0, :, LANES * pair:LANES * (pair + 1)] = (jnp.where(low, even, odd) / den).astype(BF16)


def _attention(qe, qo, ke, ko, ve, vo):
    b, s, _ = qe.shape
    tq = min(TQ, s)
    assert tq % TK == 0 and s % tq == 0
    tri = jnp.where(lax.broadcasted_iota(jnp.int32, (TK, TK), 0)
                    >= lax.broadcasted_iota(jnp.int32, (TK, TK), 1), 0.0, MASK_VALUE).astype(F32)
    q_spec = pl.BlockSpec((1, tq, ATTN_WIDTH), lambda i, j: (i, j, 0))
    kv_spec = pl.BlockSpec((1, s, ATTN_WIDTH), lambda i, j: (i, 0, 0))
    return pl.pallas_call(
        _attn_kernel,
        grid=(b, s // tq),
        in_specs=[q_spec, q_spec, kv_spec, kv_spec, kv_spec, kv_spec, _const_spec(tri.shape)],
        out_specs=q_spec,
        out_shape=jax.ShapeDtypeStruct((b, s, ATTN_WIDTH), BF16),
        scratch_shapes=[pltpu.VMEM((NUM_HEADS, tq, LANES), F32),
                        pltpu.VMEM((NUM_HEADS, tq, LANES), F32)],
        compiler_params=pltpu.CompilerParams(
            dimension_semantics=("arbitrary", "arbitrary"), vmem_limit_bytes=VMEM_LIMIT_BYTES),
        name="attention",
    )(qe, qo, ke, ko, ve, vo, tri)


def _ssm_kernel(u_ref, un_ref, bw_ref, ar_ref, ai_ref, cw_ref, cb_ref, d_ref, wglu_ref, bglu_ref, o_ref,
                st0_ref, st1_ref, ub0_ref, ub1_ref, pin0_ref, pin1_ref, pout_ref, x_ref):
    nb, lt2, width = u_ref.shape
    lt = lt2 // 2
    npair = lt // 2
    prow = nb * npair
    n_set = width // LANES
    set_w = 2 * N_STATES // n_set
    half_set = set_w // 2

    def to_pairs(u, perm_ref):
        uf = u.astype(F32)
        for s in range(n_set):
            for b in range(nb):
                perm_ref[s, b * SSM_PITCH:b * SSM_PITCH + lt, :] = uf[b, :, LANES * s:LANES * (s + 1)]
        step = lambda t, s: perm_ref[s, pl.ds(t, nb, stride=SSM_PITCH), :]
        rows = [jnp.concatenate([step(2 * k + par, s) for s in range(n_set) for par in range(2)], axis=1)
                for k in range(npair)]
        return jnp.concatenate(rows, axis=0).astype(BF16)

    def store_seq_major(y, t0):
        for k in range(npair):
            for par in range(2):
                src = y[par * prow + nb * k:par * prow + nb * (k + 1)]
                for s in range(n_set):
                    pout_ref[s, pl.ds(2 * k + par, nb, stride=SSM_PITCH), :] = src[:, LANES * s:LANES * (s + 1)]
        for b in range(nb):
            o_ref[b, t0:t0 + lt, :] = jnp.concatenate(
                [pout_ref[s, b * SSM_PITCH:b * SSM_PITCH + lt, :] for s in range(n_set)],
                axis=1).astype(BF16)

    def project_in(ub_ref, st_ref):
        for j in range(n_set):
            st_ref[nb:nb + prow, set_w * j:set_w * (j + 1)] = _dot(
                ub_ref[:, 2 * LANES * j:2 * LANES * (j + 1)], bw_ref[j])

    def scan(st_ref):
        st_ref[0:nb, :] = x_ref[...]
        vregs_per_set = half_set // LANES
        for chunk in range(n_set * vregs_per_set // SCAN_SLABS):
            cols = []
            for v in range(chunk * SCAN_SLABS, (chunk + 1) * SCAN_SLABS):
                j, q = divmod(v, vregs_per_set)
                cols.append((set_w * j + LANES * q, set_w * j + half_set + LANES * q, LANES * v))
            ar = [jnp.broadcast_to(ar_ref[:, n:n + LANES], (nb, LANES)) for _, _, n in cols]
            ai = [jnp.broadcast_to(ai_ref[:, n:n + LANES], (nb, LANES)) for _, _, n in cols]
            xr = [st_ref[0:nb, rc:rc + LANES] for rc, _, _ in cols]
            xi = [st_ref[0:nb, ic:ic + LANES] for _, ic, _ in cols]
            for k in range(npair):
                rows = slice(nb * (k + 1), nb * (k + 2))
                for s, (rc, ic, _) in enumerate(cols):
                    r = ar[s] * xr[s] - ai[s] * xi[s] + st_ref[rows, rc:rc + LANES]
                    m = ar[s] * xi[s] + ai[s] * xr[s] + st_ref[rows, ic:ic + LANES]
                    st_ref[rows, rc:rc + LANES] = r
                    st_ref[rows, ic:ic + LANES] = m
                    xr[s], xi[s] = r, m
        x_ref[...] = st_ref[prow:prow + nb, :]

    def project_out(st_ref, ub_ref):
        y2 = [_dot(st_ref[:, set_w * j:set_w * (j + 1)].astype(BF16), cw_ref[j]) for j in range(n_set)]
        y_odd = jnp.concatenate([y[nb:, 0:LANES] for y in y2], axis=1)
        from_state = jnp.concatenate([y[:prow, LANES:2 * LANES] for y in y2], axis=1)
        u_even = jnp.concatenate(
            [ub_ref[:, 2 * LANES * s:2 * LANES * s + LANES] for s in range(n_set)], axis=1)
        u_odd = jnp.concatenate(
            [ub_ref[:, 2 * LANES * s + LANES:2 * LANES * (s + 1)] for s in range(n_set)], axis=1)
        half_w = width // 2
        direct = jnp.concatenate(
            [_dot(u_even[:, half_w * h:half_w * (h + 1)], cb_ref[h]) for h in range(2)], axis=1)
        y = jnp.concatenate([from_state + direct + d_ref[...] * u_even.astype(F32),
                             y_odd + d_ref[...] * u_odd.astype(F32)], axis=0)
        y = jax.nn.gelu(y, approximate=True)
        return y * jax.nn.sigmoid(_dot(y.astype(BF16), wglu_ref[...]) + bglu_ref[...])

    @pl.when(pl.program_id(1) == 0)
    def _():
        x_ref[...] = jnp.zeros_like(x_ref)
        ub0_ref[...] = to_pairs(u_ref[:, 0:lt, :], pin0_ref)
        project_in(ub0_ref, st0_ref)

    ub1_ref[...] = to_pairs(u_ref[:, lt:lt2, :], pin1_ref)
    project_in(ub1_ref, st1_ref)
    scan(st0_ref)
    store_seq_major(project_out(st0_ref, ub0_ref), 0)
    ub0_ref[...] = to_pairs(un_ref[...], pin0_ref)
    project_in(ub0_ref, st0_ref)
    scan(st1_ref)
    store_seq_major(project_out(st1_ref, ub1_ref), lt)


def _ssm(u, layer, bw, ar, ai, cw, cb, d, wglu, bglu):
    b, s, _ = u.shape
    assert b % SSM_BATCH == 0 and s % (2 * LT) == 0 and LT % 2 == 0
    last_tile = s // LT - 1
    prow = SSM_BATCH * LT // 2
    tok = pl.BlockSpec((SSM_BATCH, 2 * LT, SSM_WIDTH), lambda i, j: (i, j, 0))
    nxt = pl.BlockSpec((SSM_BATCH, LT, SSM_WIDTH), lambda i, j: (i, jnp.minimum(2 * j + 2, last_tile), 0))
    state_buf = pltpu.VMEM((SSM_BATCH + prow, 2 * N_STATES), F32)
    u_buf = pltpu.VMEM((prow, 2 * SSM_WIDTH), BF16)
    perm_buf = pltpu.VMEM((SSM_WIDTH // LANES, SSM_BATCH * SSM_PITCH, LANES), F32)
    consts = (bw, ar, ai, cw, cb, d, wglu, bglu)
    return pl.pallas_call(
        _ssm_kernel,
        grid=(b // SSM_BATCH, s // (2 * LT)),
        in_specs=[tok, nxt] + [_const_spec(c.shape, layer) for c in consts],
        out_specs=tok,
        out_shape=jax.ShapeDtypeStruct(u.shape, BF16),
        scratch_shapes=[state_buf, state_buf, u_buf, u_buf, perm_buf, perm_buf, perm_buf,
                        pltpu.VMEM((SSM_BATCH, 2 * N_STATES), F32)],
        compiler_params=pltpu.CompilerParams(
            dimension_semantics=("arbitrary", "arbitrary"), vmem_limit_bytes=VMEM_LIMIT_BYTES),
        name="s5",
    )(u, u, *consts)


def _merge_mlp_kernel(x_ref, ya_ref, yb_ref, ga_ref, gb_ref, wa_ref, wb_ref, wo_ref, g_ref,
                      wup_ref, wdn_ref, gfin_ref, o_ref, *, final_norm):
    mixed = (ga_ref[...].astype(F32) * _dot(ya_ref[...], wa_ref[...])
             + gb_ref[...].astype(F32) * _dot(yb_ref[...], wb_ref[...]))
    x = x_ref[...] + _dot(mixed.astype(BF16), wo_ref[...])
    h = _rmsnorm(x, g_ref[...]).astype(BF16)
    for c in range(D_FF // FF_CHUNK):
        cs = slice(c * FF_CHUNK, (c + 1) * FF_CHUNK)
        up = jnp.maximum(_dot(h, wup_ref[:, cs]), 0.0)
        x = x + _dot((up * up).astype(BF16), wdn_ref[cs, :])
    if final_norm:
        x = _rmsnorm(x, gfin_ref[...])
    o_ref[...] = x


def _merge_mlp(x, ya, yb, ga, gb, layer, wa, wb, wo, g, wup, wdn, gfin, final_norm):
    t = x.shape[0]
    tm = min(TM_MLP, t)
    tok = lambda width: pl.BlockSpec((tm, width), lambda i: (i, 0))
    return pl.pallas_call(
        functools.partial(_merge_mlp_kernel, final_norm=final_norm),
        grid=(t // tm,),
        in_specs=[tok(D_MODEL), tok(ATTN_WIDTH), tok(SSM_WIDTH), tok(D_MODEL), tok(D_MODEL),
                  _const_spec(wa.shape, layer), _const_spec(wb.shape, layer),
                  _const_spec(wo.shape, layer), _const_spec(g.shape, layer),
                  _const_spec(wup.shape, layer), _const_spec(wdn.shape, layer),
                  _const_spec(gfin.shape)],
        out_specs=tok(D_MODEL),
        out_shape=jax.ShapeDtypeStruct(x.shape, F32),
        compiler_params=pltpu.CompilerParams(
            dimension_semantics=("arbitrary",), vmem_limit_bytes=VMEM_LIMIT_BYTES),
        name="merge_mlp",
    )(x, ya, yb, ga, gb, wa, wb, wo, g, wup, wdn, gfin)


def _ssm_params(lam_re, lam_im, log_dt, b_re, b_im, c_re, c_im):
    depth = lam_re.shape[0]
    dt = jnp.exp(log_dt)[..., None]
    mag = jnp.exp(lam_re * dt)
    ar = mag * jnp.cos(lam_im * dt)
    ai = mag * jnp.sin(lam_im * dt)
    den = lam_re * lam_re + lam_im * lam_im
    zr = ((ar - 1.0) * lam_re + ai * lam_im) / den
    zi = (ai * lam_re - (ar - 1.0) * lam_im) / den
    bb_re = zr[..., None] * b_re - zi[..., None] * b_im
    bb_im = zr[..., None] * b_im + zi[..., None] * b_re

    a2r = ar * ar - ai * ai
    a2i = 2.0 * ar * ai
    abb_re = ar[..., None] * bb_re - ai[..., None] * bb_im
    abb_im = ar[..., None] * bb_im + ai[..., None] * bb_re
    ca_re = c_re * ar[:, :, None, :] - c_im * ai[:, :, None, :]
    ca_im = c_re * ai[:, :, None, :] + c_im * ar[:, :, None, :]
    cb = (jnp.einsum("lgop,lgpi->lgio", c_re, bb_re) - jnp.einsum("lgop,lgpi->lgio", c_im, bb_im))

    n_set = SSM_WIDTH // LANES
    gs = SSM_GROUPS // n_set
    ch, ns = SSM_GROUP_CH, SSM_STATE

    def block_diag(quad, rows, cols, spec):
        m = jnp.stack([jnp.stack(pair) for pair in quad]).reshape(2, 2, depth, n_set, gs, *quad[0][0].shape[2:])
        m = jnp.einsum(spec, m, jnp.eye(gs, dtype=F32))
        m = m.reshape(2, 2, depth, n_set, gs * rows, gs * cols)
        return m.transpose(2, 3, 0, 4, 1, 5).reshape(depth, n_set, 2 * gs * rows, 2 * gs * cols)

    bw = block_diag([[abb_re, abb_im], [bb_re, bb_im]], ch, ns, "xylsgpc,gh->xylsgchp")
    cw = block_diag([[c_re, ca_re], [-c_im, -ca_im]], ns, ch, "xylsgcp,gh->xylsgphc")
    hg = SSM_GROUPS // 2
    cbw = jnp.einsum("lsgio,gh->lsgiho", cb.reshape(depth, 2, hg, ch, ch),
                     jnp.eye(hg, dtype=F32)).reshape(depth, 2, hg * ch, hg * ch)
    return (bw.astype(BF16), a2r.reshape(depth, 1, N_STATES), a2i.reshape(depth, 1, N_STATES),
            cw.astype(BF16), cbw.astype(BF16))


def _split_w_in(w_in, b_forget):
    depth = w_in.shape[0]
    o1 = ATTN_WIDTH
    o2 = o1 + ATTN_WIDTH
    o3 = o2 + ATTN_WIDTH
    o4 = o3 + NUM_HEADS
    o5 = o4 + SSM_WIDTH
    o6 = o5 + D_MODEL
    wq = (w_in[:, :, :o1] * (LOG2E * HEAD_DIM ** -0.5)).astype(BF16)
    reps = BIAS_PIECES * NUM_HEADS
    wf = jnp.zeros((depth, D_MODEL, LANES), F32).at[:, :, :reps].set(
        jnp.tile(w_in[:, :, o3:o4], (1, 1, BIAS_PIECES))).astype(BF16)
    bf = jnp.zeros((depth, 1, LANES), F32).at[:, 0, :reps].set(jnp.tile(b_forget, (1, BIAS_PIECES)))
    return (wq, w_in[:, :, o1:o2].astype(BF16), w_in[:, :, o2:o3].astype(BF16), wf, bf,
            w_in[:, :, o4:o5].astype(BF16), w_in[:, :, o5:o6].astype(BF16),
            w_in[:, :, o6:].astype(BF16))


@jax.jit
def kernel(x, norm_mix, w_in, b_forget, ssm_lambda_re, ssm_lambda_im, ssm_log_dt, ssm_b_re, ssm_b_im,
           ssm_c_re, ssm_c_im, ssm_d, w_glu, b_glu, w_branch_a, w_branch_b, w_out, norm_mlp,
           w_mlp_up, w_mlp_down, norm_final):
    b, s, d = x.shape
    depth = w_in.shape[0]
    t = b * s
    proj = (norm_mix.reshape(depth, 1, d),) + _split_w_in(w_in, b_forget)
    ssm = _ssm_params(ssm_lambda_re, ssm_lambda_im, ssm_log_dt, ssm_b_re, ssm_b_im, ssm_c_re, ssm_c_im)
    ssm = ssm + (ssm_d.reshape(depth, 1, SSM_WIDTH), w_glu.astype(BF16), b_glu.reshape(depth, 1, SSM_WIDTH))
    mlp = (w_branch_a.astype(BF16), w_branch_b.astype(BF16), w_out.astype(BF16),
           norm_mlp.reshape(depth, 1, d), w_mlp_up.astype(BF16), w_mlp_down.astype(BF16))
    gfin = norm_final.reshape(1, d)
    for l in range(depth):
        qe, qo, ke, ko, ve, vo, u, ga, gb = _inproj(x, l, *proj)
        ya = _attention(qe, qo, ke, ko, ve, vo)
        yb = _ssm(u, l, *ssm)
        x = _merge_mlp(x.reshape(t, d), ya.reshape(t, ATTN_WIDTH), yb.reshape(t, SSM_WIDTH),
                       ga.reshape(t, d), gb.reshape(t, d), l, *mlp, gfin,
                       final_norm=(l == depth - 1)).reshape(b, s, d)
    return x
```

```python
import functools
import math

import jax
import jax.numpy as jnp
from jax import lax
from jax.experimental import pallas as pl
from jax.experimental.pallas import tpu as pltpu

F32 = jnp.float32
BF16 = jnp.bfloat16

D_MODEL = 1024
NUM_HEADS = 8
HEAD_DIM = 64
ATTN_WIDTH = NUM_HEADS * HEAD_DIM
SSM_GROUPS = 32
SSM_GROUP_CH = 16
SSM_STATE = 64
SSM_WIDTH = SSM_GROUPS * SSM_GROUP_CH
N_STATES = SSM_GROUPS * SSM_STATE
D_FF = 4 * D_MODEL
RMS_EPS = 1e-6
MASK_VALUE = -1e30
LOG2E = math.log2(math.e)
BIAS_PIECES = 3
GELU_K1 = -2.0 * math.sqrt(2.0 / math.pi) * LOG2E
GELU_K3 = 0.044715 * GELU_K1

LANES = 128
SUBLANES = 8
VMEM_LIMIT_BYTES = 56 * 1024 * 1024

TM_PROJ = 1024
TQ = 1024
TK = 256
LT = 128
SSM_BATCH = SUBLANES
SSM_PITCH = LT + SUBLANES
SCAN_SLABS = 16
TM_MLP = 512
FF_CHUNK = 1024

_dot = functools.partial(jnp.dot, preferred_element_type=F32)


def _rmsnorm(x, g):
    ms = jnp.mean(x * x, axis=-1, keepdims=True)
    return x * lax.rsqrt(ms + RMS_EPS) * g


def _const_spec(shape, layer=None):
    if layer is None:
        zeros = (0,) * len(shape)
        return pl.BlockSpec(shape, lambda *_: zeros, pipeline_mode=pl.Buffered(1))
    index = (layer,) + (0,) * (len(shape) - 1)
    return pl.BlockSpec((None,) + tuple(shape[1:]), lambda *_: index, pipeline_mode=pl.Buffered(1))


def _low_half(shape):
    lane = lax.broadcasted_iota(jnp.int32, shape, len(shape) - 1)
    return (lane & (LANES - 1)) < HEAD_DIM


def _cumsum_rows(x):
    n = x.shape[0]
    row = lax.broadcasted_iota(jnp.int32, x.shape, 0)
    k = 1
    while k < n:
        x = x + jnp.where(row >= k, pltpu.roll(x, k, axis=0), 0.0)
        k *= 2
    return x


def _inproj_kernel(x_ref, g_ref, wq_ref, wk_ref, wv_ref, wf_ref, bf_ref, pe_ref, po_ref, one_ref,
                   wu_ref, wga_ref, wgb_ref,
                   qe_ref, qo_ref, ke_ref, ko_ref, ve_ref, vo_ref, u_ref, ga_ref, gb_ref, carry_ref):
    @pl.when(pl.program_id(1) == 0)
    def _():
        carry_ref[...] = jnp.zeros_like(carry_ref)

    tm = x_ref.shape[1]
    h = _rmsnorm(x_ref[0], g_ref[...]).astype(BF16)
    low = _low_half((tm, ATTN_WIDTH))
    q_ones = one_ref[0:1, :]
    q_ones_odd = one_ref[1:2, :]

    q = _dot(h, wq_ref[...])
    qe_ref[0] = jnp.where(low, q, q_ones).astype(BF16)
    qo_ref[0] = jnp.where(low, q_ones_odd, q).astype(BF16)
    v = _dot(h, wv_ref[...])
    ve_ref[0] = jnp.where(low, v, 1.0).astype(BF16)
    vo_ref[0] = jnp.where(low, 1.0, v).astype(BF16)

    fl = _dot(h, wf_ref[...]) + bf_ref[...]
    log_f = jnp.minimum(fl, 0.0) - jnp.log1p(jnp.exp(-jnp.abs(fl)))
    cum = _cumsum_rows(log_f) + carry_ref[0:1, :]
    carry_ref[...] = jnp.broadcast_to(cum[tm - 1:tm, :], carry_ref.shape)
    bias = cum * (-LOG2E)
    hi = bias.astype(BF16).astype(F32)
    mid = (bias - hi).astype(BF16).astype(F32)
    lo = (bias - hi - mid).astype(BF16).astype(F32)
    lane = lax.broadcasted_iota(jnp.int32, bias.shape, 1)
    pieces = jnp.where(lane < NUM_HEADS, hi, jnp.where(lane < 2 * NUM_HEADS, mid, lo)).astype(BF16)
    k = _dot(h, wk_ref[...])
    ke_ref[0] = jnp.where(low, k, _dot(pieces, pe_ref[...])).astype(BF16)
    ko_ref[0] = jnp.where(low, _dot(pieces, po_ref[...]), k).astype(BF16)

    u_ref[0] = _dot(h, wu_ref[...]).astype(BF16)
    ga_ref[0] = jax.nn.sigmoid(_dot(h, wga_ref[...])).astype(BF16)
    gb_ref[0] = jax.nn.sigmoid(_dot(h, wgb_ref[...])).astype(BF16)


def _attn_constants():
    pe = [[0.0] * ATTN_WIDTH for _ in range(LANES)]
    po = [[0.0] * ATTN_WIDTH for _ in range(LANES)]
    ones = [[0.0] * ATTN_WIDTH for _ in range(SUBLANES)]
    for pair in range(NUM_HEADS // 2):
        for i in range(BIAS_PIECES):
            pe[2 * pair + NUM_HEADS * i][LANES * pair + HEAD_DIM + i] = 1.0
            po[2 * pair + 1 + NUM_HEADS * i][LANES * pair + i] = 1.0
            ones[0][LANES * pair + HEAD_DIM + i] = 1.0
            ones[1][LANES * pair + i] = 1.0
    return jnp.array(pe, BF16), jnp.array(po, BF16), jnp.array(ones, F32)


def _inproj(x, layer, g, wq, wk, wv, wf, bf, wu, wga, wgb):
    b, s, _ = x.shape
    tm = min(TM_PROJ, s)
    pe, po, ones = _attn_constants()
    tok = lambda width: pl.BlockSpec((1, tm, width), lambda i, j: (i, j, 0))
    act = lambda width: jax.ShapeDtypeStruct((b, s, width), BF16)
    consts = (g, wq, wk, wv, wf, bf, pe, po, ones, wu, wga, wgb)
    shared = (pe, po, ones)
    return pl.pallas_call(
        _inproj_kernel,
        grid=(b, s // tm),
        in_specs=[tok(D_MODEL)] + [
            _const_spec(c.shape, None if any(c is t for t in shared) else layer) for c in consts],
        out_specs=[tok(ATTN_WIDTH)] * 6 + [tok(SSM_WIDTH), tok(D_MODEL), tok(D_MODEL)],
        out_shape=[act(ATTN_WIDTH)] * 6 + [act(SSM_WIDTH), act(D_MODEL), act(D_MODEL)],
        scratch_shapes=[pltpu.VMEM((SUBLANES, LANES), F32)],
        compiler_params=pltpu.CompilerParams(
            dimension_semantics=("arbitrary", "arbitrary"), vmem_limit_bytes=VMEM_LIMIT_BYTES),
        name="inproj",
    )(x, *consts)


def _attn_kernel(qe_ref, qo_ref, ke_ref, ko_ref, ve_ref, vo_ref, tri_ref, o_ref, m_ref, acc_ref):
    i = pl.program_id(1)
    tq = o_ref.shape[1]

    def block(j, row0, diagonal, init):
        keys = pl.ds(pl.multiple_of(j * TK, TK), TK)
        rows = slice(row0, tq)
        for head in range(NUM_HEADS):
            ls = slice(LANES * (head // 2), LANES * (head // 2 + 1))
            q_ref, k_ref, v_ref = (qe_ref, ke_ref, ve_ref) if head % 2 == 0 else (qo_ref, ko_ref, vo_ref)
            s = lax.dot_general(q_ref[0, rows, ls], k_ref[0, keys, ls], (((1,), (1,)), ((), ())),
                                preferred_element_type=F32)
            if diagonal:
                top = s[:TK] + tri_ref[...]
                s = top if tq - row0 == TK else jnp.concatenate([top, s[TK:]], axis=0)
            s_cols = [s[:, LANES * c:LANES * (c + 1)] for c in range(TK // LANES)]
            m_blk = jnp.max(functools.reduce(jnp.maximum, s_cols), axis=-1, keepdims=True)
            if init:
                m_new = jnp.broadcast_to(m_blk, (tq - row0, LANES))
            else:
                m_old = m_ref[head, rows, :]
                m_new = jnp.maximum(m_old, m_blk)
            p = jnp.concatenate([jnp.exp2((sc - m_new).astype(BF16)) for sc in s_cols], axis=1)
            pv = _dot(p, v_ref[0, keys, ls])
            if init:
                acc_ref[head, rows, :] = pv
            else:
                acc_ref[head, rows, :] = jnp.exp2(m_old - m_new) * acc_ref[head, rows, :] + pv
            m_ref[head, rows, :] = m_new

    blocks_per_step = tq // TK
    for c in range(blocks_per_step):
        block(i * blocks_per_step + c, TK * c, True, init=(c == 0))

    def body(jj, carry):
        for c in range(blocks_per_step):
            block(jj * blocks_per_step + c, 0, False, init=False)
        return carry

    lax.fori_loop(0, i, body, 0)

    low = _low_half((tq, LANES))
    for pair in range(NUM_HEADS // 2):
        even = acc_ref[2 * pair]
        odd = acc_ref[2 * pair + 1]
        den = pltpu.roll(jnp.where(low, odd, even), HEAD_DIM, axis=1)
        o_ref[0, :, LANES * pair:LANES * (pair + 1)] = (jnp.where(low, even, odd) / den).astype(BF16)


def _attention(qe, qo, ke, ko, ve, vo):
    b, s, _ = qe.shape
    tq = min(TQ, s)
    assert tq % TK == 0 and s % tq == 0
    tri = jnp.where(lax.broadcasted_iota(jnp.int32, (TK, TK), 0)
                    >= lax.broadcasted_iota(jnp.int32, (TK, TK), 1), 0.0, MASK_VALUE).astype(F32)
    q_spec = pl.BlockSpec((1, tq, ATTN_WIDTH), lambda i, j: (i, j, 0))
    kv_spec = pl.BlockSpec((1, s, ATTN_WIDTH), lambda i, j: (i, 0, 0))
    return pl.pallas_call(
        _attn_kernel,
        grid=(b, s // tq),
        in_specs=[q_spec, q_spec, kv_spec, kv_spec, kv_spec, kv_spec, _const_spec(tri.shape)],
        out_specs=q_spec,
        out_shape=jax.ShapeDtypeStruct((b, s, ATTN_WIDTH), BF16),
        scratch_shapes=[pltpu.VMEM((NUM_HEADS, tq, LANES), F32),
                        pltpu.VMEM((NUM_HEADS, tq, LANES), F32)],
        compiler_params=pltpu.CompilerParams(
            dimension_semantics=("arbitrary", "arbitrary"), vmem_limit_bytes=VMEM_LIMIT_BYTES),
        name="attention",
    )(qe, qo, ke, ko, ve, vo, tri)


def _ssm_kernel(u_ref, un_ref, bw_ref, ar_ref, ai_ref, cw_ref, cb_ref, d_ref, wglu_ref, bglu_ref, o_ref,
                st0_ref, st1_ref, ub0_ref, ub1_ref, pin0_ref, pin1_ref, pout_ref, x_ref):
    nb, lt2, width = u_ref.shape
    lt = lt2 // 2
    npair = lt // 2
    prow = nb * npair
    n_set = width // LANES
    set_w = 2 * N_STATES // n_set
    half_set = set_w // 2

    def to_pairs(u, perm_ref):
        uf = u.astype(F32)
        for s in range(n_set):
            for b in range(nb):
                perm_ref[s, b * SSM_PITCH:b * SSM_PITCH + lt, :] = uf[b, :, LANES * s:LANES * (s + 1)]
        step = lambda t, s: perm_ref[s, pl.ds(t, nb, stride=SSM_PITCH), :]
        rows = [jnp.concatenate([step(2 * k + par, s) for s in range(n_set) for par in range(2)], axis=1)
                for k in range(npair)]
        return jnp.concatenate(rows, axis=0).astype(BF16)

    def store_seq_major(y, t0):
        for k in range(npair):
            for par in range(2):
                src = y[par * prow + nb * k:par * prow + nb * (k + 1)]
                for s in range(n_set):
                    pout_ref[s, pl.ds(2 * k + par, nb, stride=SSM_PITCH), :] = src[:, LANES * s:LANES * (s + 1)]
        for b in range(nb):
            o_ref[b, t0:t0 + lt, :] = jnp.concatenate(
                [pout_ref[s, b * SSM_PITCH:b * SSM_PITCH + lt, :] for s in range(n_set)],
                axis=1).astype(BF16)

    def project_in(ub_ref, st_ref):
        for j in range(n_set):
            st_ref[nb:nb + prow, set_w * j:set_w * (j + 1)] = _dot(
                ub_ref[:, 2 * LANES * j:2 * LANES * (j + 1)], bw_ref[j])

    def scan(st_ref):
        st_ref[0:nb, :] = x_ref[...]
        vregs_per_set = half_set // LANES
        for chunk in range(n_set * vregs_per_set // SCAN_SLABS):
            cols = []
            for v in range(chunk * SCAN_SLABS, (chunk + 1) * SCAN_SLABS):
                j, q = divmod(v, vregs_per_set)
                cols.append((set_w * j + LANES * q, set_w * j + half_set + LANES * q, LANES * v))
            ar = [jnp.broadcast_to(ar_ref[:, n:n + LANES], (nb, LANES)) for _, _, n in cols]
            ai = [jnp.broadcast_to(ai_ref[:, n:n + LANES], (nb, LANES)) for _, _, n in cols]
            xr = [st_ref[0:nb, rc:rc + LANES] for rc, _, _ in cols]
            xi = [st_ref[0:nb, ic:ic + LANES] for _, ic, _ in cols]
            for k in range(npair):
                rows = slice(nb * (k + 1), nb * (k + 2))
                for s, (rc, ic, _) in enumerate(cols):
                    r = ar[s] * xr[s] - ai[s] * xi[s] + st_ref[rows, rc:rc + LANES]
                    m = ar[s] * xi[s] + ai[s] * xr[s] + st_ref[rows, ic:ic + LANES]
                    st_ref[rows, rc:rc + LANES] = r
                    st_ref[rows, ic:ic + LANES] = m
                    xr[s], xi[s] = r, m
        x_ref[...] = st_ref[prow:prow + nb, :]

    def project_out(st_ref, ub_ref):
        y2 = [_dot(st_ref[:, set_w * j:set_w * (j + 1)].astype(BF16), cw_ref[j]) for j in range(n_set)]
        y_odd = jnp.concatenate([y[nb:, 0:LANES] for y in y2], axis=1)
        from_state = jnp.concatenate([y[:prow, LANES:2 * LANES] for y in y2], axis=1)
        u_even = jnp.concatenate(
            [ub_ref[:, 2 * LANES * s:2 * LANES * s + LANES] for s in range(n_set)], axis=1)
        u_odd = jnp.concatenate(
            [ub_ref[:, 2 * LANES * s + LANES:2 * LANES * (s + 1)] for s in range(n_set)], axis=1)
        half_w = width // 2
        direct = jnp.concatenate(
            [_dot(u_even[:, half_w * h:half_w * (h + 1)], cb_ref[h]) for h in range(2)], axis=1)
        y = jnp.concatenate([from_state + direct + d_ref[...] * u_even.astype(F32),
                             y_odd + d_ref[...] * u_odd.astype(F32)], axis=0)
        y = y / (1.0 + jnp.exp2(y * (GELU_K1 + GELU_K3 * (y * y))))
        return y * jax.nn.sigmoid(_dot(y.astype(BF16), wglu_ref[...]) + bglu_ref[...])

    @pl.when(pl.program_id(1) == 0)
    def _():
        x_ref[...] = jnp.zeros_like(x_ref)
        ub0_ref[...] = to_pairs(u_ref[:, 0:lt, :], pin0_ref)
        project_in(ub0_ref, st0_ref)

    ub1_ref[...] = to_pairs(u_ref[:, lt:lt2, :], pin1_ref)
    project_in(ub1_ref, st1_ref)
    scan(st0_ref)
    store_seq_major(project_out(st0_ref, ub0_ref), 0)
    ub0_ref[...] = to_pairs(un_ref[...], pin0_ref)
    project_in(ub0_ref, st0_ref)
    scan(st1_ref)
    store_seq_major(project_out(st1_ref, ub1_ref), lt)


def _ssm(u, layer, bw, ar, ai, cw, cb, d, wglu, bglu):
    b, s, _ = u.shape
    assert b % SSM_BATCH == 0 and s % (2 * LT) == 0 and LT % 2 == 0
    last_tile = s // LT - 1
    prow = SSM_BATCH * LT // 2
    tok = pl.BlockSpec((SSM_BATCH, 2 * LT, SSM_WIDTH), lambda i, j: (i, j, 0))
    nxt = pl.BlockSpec((SSM_BATCH, LT, SSM_WIDTH), lambda i, j: (i, jnp.minimum(2 * j + 2, last_tile), 0))
    state_buf = pltpu.VMEM((SSM_BATCH + prow, 2 * N_STATES), F32)
    u_buf = pltpu.VMEM((prow, 2 * SSM_WIDTH), BF16)
    perm_buf = pltpu.VMEM((SSM_WIDTH // LANES, SSM_BATCH * SSM_PITCH, LANES), F32)
    consts = (bw, ar, ai, cw, cb, d, wglu, bglu)
    return pl.pallas_call(
        _ssm_kernel,
        grid=(b // SSM_BATCH, s // (2 * LT)),
        in_specs=[tok, nxt] + [_const_spec(c.shape, layer) for c in consts],
        out_specs=tok,
        out_shape=jax.ShapeDtypeStruct(u.shape, BF16),
        scratch_shapes=[state_buf, state_buf, u_buf, u_buf, perm_buf, perm_buf, perm_buf,
                        pltpu.VMEM((SSM_BATCH, 2 * N_STATES), F32)],
        compiler_params=pltpu.CompilerParams(
            dimension_semantics=("arbitrary", "arbitrary"), vmem_limit_bytes=VMEM_LIMIT_BYTES),
        name="s5",
    )(u, u, *consts)


def _merge_mlp_kernel(x_ref, ya_ref, yb_ref, ga_ref, gb_ref, wa_ref, wb_ref, wo_ref, g_ref,
                      wup_ref, wdn_ref, gfin_ref, o_ref, *, final_norm):
    mixed = (ga_ref[...].astype(F32) * _dot(ya_ref[...], wa_ref[...])
             + gb_ref[...].astype(F32) * _dot(yb_ref[...], wb_ref[...]))
    x = x_ref[...] + _dot(mixed.astype(BF16), wo_ref[...])
    h = _rmsnorm(x, g_ref[...]).astype(BF16)
    for c in range(D_FF // FF_CHUNK):
        cs = slice(c * FF_CHUNK, (c + 1) * FF_CHUNK)
        up = jnp.maximum(_dot(h, wup_ref[:, cs]), 0.0)
        x = x + _dot((up * up).astype(BF16), wdn_ref[cs, :])
    if final_norm:
        x = _rmsnorm(x, gfin_ref[...])
    o_ref[...] = x


def _merge_mlp(x, ya, yb, ga, gb, layer, wa, wb, wo, g, wup, wdn, gfin, final_norm):
    t = x.shape[0]
    tm = min(TM_MLP, t)
    tok = lambda width: pl.BlockSpec((tm, width), lambda i: (i, 0))
    return pl.pallas_call(
        functools.partial(_merge_mlp_kernel, final_norm=final_norm),
        grid=(t // tm,),
        in_specs=[tok(D_MODEL), tok(ATTN_WIDTH), tok(SSM_WIDTH), tok(D_MODEL), tok(D_MODEL),
                  _const_spec(wa.shape, layer), _const_spec(wb.shape, layer),
                  _const_spec(wo.shape, layer), _const_spec(g.shape, layer),
                  _const_spec(wup.shape, layer), _const_spec(wdn.shape, layer),
                  _const_spec(gfin.shape)],
        out_specs=tok(D_MODEL),
        out_shape=jax.ShapeDtypeStruct(x.shape, F32),
        compiler_params=pltpu.CompilerParams(
            dimension_semantics=("arbitrary",), vmem_limit_bytes=VMEM_LIMIT_BYTES),
        name="merge_mlp",
    )(x, ya, yb, ga, gb, wa, wb, wo, g, wup, wdn, gfin)


def _ssm_params(lam_re, lam_im, log_dt, b_re, b_im, c_re, c_im):
    depth = lam_re.shape[0]
    dt = jnp.exp(log_dt)[..., None]
    mag = jnp.exp(lam_re * dt)
    ar = mag * jnp.cos(lam_im * dt)
    ai = mag * jnp.sin(lam_im * dt)
    den = lam_re * lam_re + lam_im * lam_im
    zr = ((ar - 1.0) * lam_re + ai * lam_im) / den
    zi = (ai * lam_re - (ar - 1.0) * lam_im) / den
    bb_re = zr[..., None] * b_re - zi[..., None] * b_im
    bb_im = zr[..., None] * b_im + zi[..., None] * b_re

    a2r = ar * ar - ai * ai
    a2i = 2.0 * ar * ai
    abb_re = ar[..., None] * bb_re - ai[..., None] * bb_im
    abb_im = ar[..., None] * bb_im + ai[..., None] * bb_re
    ca_re = c_re * ar[:, :, None, :] - c_im * ai[:, :, None, :]
    ca_im = c_re * ai[:, :, None, :] + c_im * ar[:, :, None, :]
    cb = (jnp.einsum("lgop,lgpi->lgio", c_re, bb_re) - jnp.einsum("lgop,lgpi->lgio", c_im, bb_im))

    n_set = SSM_WIDTH // LANES
    gs = SSM_GROUPS // n_set
    ch, ns = SSM_GROUP_CH, SSM_STATE

    def block_diag(quad, rows, cols, spec):
        m = jnp.stack([jnp.stack(pair) for pair in quad]).astype(BF16)
        m = m.reshape(2, 2, depth, n_set, gs, *quad[0][0].shape[2:])
        m = jnp.einsum(spec, m, jnp.eye(gs, dtype=BF16))
        m = m.reshape(2, 2, depth, n_set, gs * rows, gs * cols)
        return m.transpose(2, 3, 0, 4, 1, 5).reshape(depth, n_set, 2 * gs * rows, 2 * gs * cols)

    bw = block_diag([[abb_re, abb_im], [bb_re, bb_im]], ch, ns, "xylsgpc,gh->xylsgchp")
    cw = block_diag([[c_re, ca_re], [-c_im, -ca_im]], ns, ch, "xylsgcp,gh->xylsgphc")
    hg = SSM_GROUPS // 2
    cbw = jnp.einsum("lsgio,gh->lsgiho", cb.reshape(depth, 2, hg, ch, ch),
                     jnp.eye(hg, dtype=F32)).reshape(depth, 2, hg * ch, hg * ch)
    return (bw.astype(BF16), a2r.reshape(depth, 1, N_STATES), a2i.reshape(depth, 1, N_STATES),
            cw.astype(BF16), cbw.astype(BF16))


def _split_w_in(w_in, b_forget):
    depth = w_in.shape[0]
    o1 = ATTN_WIDTH
    o2 = o1 + ATTN_WIDTH
    o3 = o2 + ATTN_WIDTH
    o4 = o3 + NUM_HEADS
    o5 = o4 + SSM_WIDTH
    o6 = o5 + D_MODEL
    wq = (w_in[:, :, :o1] * (LOG2E * HEAD_DIM ** -0.5)).astype(BF16)
    reps = BIAS_PIECES * NUM_HEADS
    wf = jnp.zeros((depth, D_MODEL, LANES), F32).at[:, :, :reps].set(
        jnp.tile(w_in[:, :, o3:o4], (1, 1, BIAS_PIECES))).astype(BF16)
    bf = jnp.zeros((depth, 1, LANES), F32).at[:, 0, :reps].set(jnp.tile(b_forget, (1, BIAS_PIECES)))
    wb = w_in.astype(BF16)
    return (wq, wb[:, :, o1:o2], wb[:, :, o2:o3], wf, bf,
            wb[:, :, o4:o5], wb[:, :, o5:o6], wb[:, :, o6:])


@jax.jit
def kernel(x, norm_mix, w_in, b_forget, ssm_lambda_re, ssm_lambda_im, ssm_log_dt, ssm_b_re, ssm_b_im,
           ssm_c_re, ssm_c_im, ssm_d, w_glu, b_glu, w_branch_a, w_branch_b, w_out, norm_mlp,
           w_mlp_up, w_mlp_down, norm_final):
    b, s, d = x.shape
    depth = w_in.shape[0]
    t = b * s
    proj = (norm_mix.reshape(depth, 1, d),) + _split_w_in(w_in, b_forget)
    ssm = _ssm_params(ssm_lambda_re, ssm_lambda_im, ssm_log_dt, ssm_b_re, ssm_b_im, ssm_c_re, ssm_c_im)
    ssm = ssm + (ssm_d.reshape(depth, 1, SSM_WIDTH), w_glu.astype(BF16), b_glu.reshape(depth, 1, SSM_WIDTH))
    mlp = (w_branch_a.astype(BF16), w_branch_b.astype(BF16), w_out.astype(BF16),
           norm_mlp.reshape(depth, 1, d), w_mlp_up.astype(BF16), w_mlp_down.astype(BF16))
    gfin = norm_final.reshape(1, d)
    for l in range(depth):
        qe, qo, ke, ko, ve, vo, u, ga, gb = _inproj(x, l, *proj)
        ya = _attention(qe, qo, ke, ko, ve, vo)
        yb = _ssm(u, l, *ssm)
        x = _merge_mlp(x.reshape(t, d), ya.reshape(t, ATTN_WIDTH), yb.reshape(t, SSM_WIDTH),
                       ga.reshape(t, d), gb.reshape(t, d), l, *mlp, gfin,
                       final_norm=(l == depth - 1)).reshape(b, s, d)
    return x
```

```python
import functools
import math

import jax
import jax.numpy as jnp
from jax import lax
from jax.experimental import pallas as pl
from jax.experimental.pallas import tpu as pltpu

F32 = jnp.float32
BF16 = jnp.bfloat16

D_MODEL = 1024
NUM_HEADS = 8
HEAD_DIM = 64
ATTN_WIDTH = NUM_HEADS * HEAD_DIM
SSM_GROUPS = 32
SSM_GROUP_CH = 16
SSM_STATE = 64
SSM_WIDTH = SSM_GROUPS * SSM_GROUP_CH
N_STATES = SSM_GROUPS * SSM_STATE
D_FF = 4 * D_MODEL
RMS_EPS = 1e-6
MASK_VALUE = -1e30
LOG2E = math.log2(math.e)
BIAS_PIECES = 3
GELU_K1 = -2.0 * math.sqrt(2.0 / math.pi) * LOG2E
GELU_K3 = 0.044715 * GELU_K1

LANES = 128
SUBLANES = 8
VMEM_LIMIT_BYTES = 56 * 1024 * 1024

TM_PROJ = 1024
TQ = 1024
TK = 256
LT = 128
SSM_BATCH = SUBLANES
SSM_PITCH = LT + SUBLANES
SCAN_SLABS = 16
TM_MLP = 512
FF_CHUNK = 1024

_dot = functools.partial(jnp.dot, preferred_element_type=F32)


def _rmsnorm(x, g):
    ms = jnp.mean(x * x, axis=-1, keepdims=True)
    return x * lax.rsqrt(ms + RMS_EPS) * g


def _const_spec(shape, layer=None):
    if layer is None:
        zeros = (0,) * len(shape)
        return pl.BlockSpec(shape, lambda *_: zeros, pipeline_mode=pl.Buffered(1))
    index = (layer,) + (0,) * (len(shape) - 1)
    return pl.BlockSpec((None,) + tuple(shape[1:]), lambda *_: index, pipeline_mode=pl.Buffered(1))


def _low_half(shape):
    lane = lax.broadcasted_iota(jnp.int32, shape, len(shape) - 1)
    return (lane & (LANES - 1)) < HEAD_DIM


def _cumsum_rows(x):
    n = x.shape[0]
    row = lax.broadcasted_iota(jnp.int32, x.shape, 0)
    k = 1
    while k < n:
        x = x + jnp.where(row >= k, pltpu.roll(x, k, axis=0), 0.0)
        k *= 2
    return x


def _inproj_kernel(x_ref, g_ref, wq_ref, wk_ref, wv_ref, wf_ref, bf_ref, pe_ref, po_ref, one_ref,
                   wu_ref, wga_ref, wgb_ref,
                   qe_ref, qo_ref, ke_ref, ko_ref, ve_ref, vo_ref, u_ref, ga_ref, gb_ref, carry_ref):
    @pl.when(pl.program_id(1) == 0)
    def _():
        carry_ref[...] = jnp.zeros_like(carry_ref)

    tm = x_ref.shape[1]
    h = _rmsnorm(x_ref[0], g_ref[...]).astype(BF16)
    low = _low_half((tm, ATTN_WIDTH))
    q_ones = one_ref[0:1, :]
    q_ones_odd = one_ref[1:2, :]

    fl = _dot(h, wf_ref[...]) + bf_ref[...]
    log_f = jnp.minimum(fl, 0.0) - jnp.log1p(jnp.exp(-jnp.abs(fl)))
    cum = _cumsum_rows(log_f) + carry_ref[0:1, :]
    carry_ref[...] = jnp.broadcast_to(cum[tm - 1:tm, :], carry_ref.shape)
    bias = cum * (-LOG2E)
    hi = bias.astype(BF16).astype(F32)
    mid = (bias - hi).astype(BF16).astype(F32)
    lo = (bias - hi - mid).astype(BF16).astype(F32)
    lane = lax.broadcasted_iota(jnp.int32, bias.shape, 1)
    pieces = jnp.where(lane < NUM_HEADS, hi, jnp.where(lane < 2 * NUM_HEADS, mid, lo)).astype(BF16)

    q = _dot(h, wq_ref[...])
    qe_ref[0] = jnp.where(low, q, q_ones).astype(BF16)
    qo_ref[0] = jnp.where(low, q_ones_odd, q).astype(BF16)
    v = _dot(h, wv_ref[...])
    ve_ref[0] = jnp.where(low, v, 1.0).astype(BF16)
    vo_ref[0] = jnp.where(low, 1.0, v).astype(BF16)
    u_ref[0] = _dot(h, wu_ref[...]).astype(BF16)
    ga_ref[0] = jax.nn.sigmoid(_dot(h, wga_ref[...])).astype(BF16)
    gb_ref[0] = jax.nn.sigmoid(_dot(h, wgb_ref[...])).astype(BF16)

    k = _dot(h, wk_ref[...])
    ke_ref[0] = jnp.where(low, k, _dot(pieces, pe_ref[...])).astype(BF16)
    ko_ref[0] = jnp.where(low, _dot(pieces, po_ref[...]), k).astype(BF16)


def _attn_constants():
    pe = [[0.0] * ATTN_WIDTH for _ in range(LANES)]
    po = [[0.0] * ATTN_WIDTH for _ in range(LANES)]
    ones = [[0.0] * ATTN_WIDTH for _ in range(SUBLANES)]
    for pair in range(NUM_HEADS // 2):
        for i in range(BIAS_PIECES):
            pe[2 * pair + NUM_HEADS * i][LANES * pair + HEAD_DIM + i] = 1.0
            po[2 * pair + 1 + NUM_HEADS * i][LANES * pair + i] = 1.0
            ones[0][LANES * pair + HEAD_DIM + i] = 1.0
            ones[1][LANES * pair + i] = 1.0
    return jnp.array(pe, BF16), jnp.array(po, BF16), jnp.array(ones, F32)


def _inproj(x, layer, g, wq, wk, wv, wf, bf, wu, wga, wgb):
    b, s, _ = x.shape
    tm = min(TM_PROJ, s)
    pe, po, ones = _attn_constants()
    tok = lambda width: pl.BlockSpec((1, tm, width), lambda i, j: (i, j, 0))
    act = lambda width: jax.ShapeDtypeStruct((b, s, width), BF16)
    consts = (g, wq, wk, wv, wf, bf, pe, po, ones, wu, wga, wgb)
    shared = (pe, po, ones)
    return pl.pallas_call(
        _inproj_kernel,
        grid=(b, s // tm),
        in_specs=[tok(D_MODEL)] + [
            _const_spec(c.shape, None if any(c is t for t in shared) else layer) for c in consts],
        out_specs=[tok(ATTN_WIDTH)] * 6 + [tok(SSM_WIDTH), tok(D_MODEL), tok(D_MODEL)],
        out_shape=[act(ATTN_WIDTH)] * 6 + [act(SSM_WIDTH), act(D_MODEL), act(D_MODEL)],
        scratch_shapes=[pltpu.VMEM((SUBLANES, LANES), F32)],
        compiler_params=pltpu.CompilerParams(
            dimension_semantics=("arbitrary", "arbitrary"), vmem_limit_bytes=VMEM_LIMIT_BYTES),
        name="inproj",
    )(x, *consts)


def _attn_kernel(qe_ref, qo_ref, ke_ref, ko_ref, ve_ref, vo_ref, tri_ref, o_ref, m_ref, acc_ref):
    i = pl.program_id(1)
    tq = o_ref.shape[1]

    def block(j, row0, diagonal, init):
        keys = pl.ds(pl.multiple_of(j * TK, TK), TK)
        rows = slice(row0, tq)
        for head in range(NUM_HEADS):
            ls = slice(LANES * (head // 2), LANES * (head // 2 + 1))
            q_ref, k_ref, v_ref = (qe_ref, ke_ref, ve_ref) if head % 2 == 0 else (qo_ref, ko_ref, vo_ref)
            s = lax.dot_general(q_ref[0, rows, ls], k_ref[0, keys, ls], (((1,), (1,)), ((), ())),
                                preferred_element_type=F32)
            if diagonal:
                top = s[:TK] + tri_ref[...]
                s = top if tq - row0 == TK else jnp.concatenate([top, s[TK:]], axis=0)
            s_cols = [s[:, LANES * c:LANES * (c + 1)] for c in range(TK // LANES)]
            m_blk = jnp.max(functools.reduce(jnp.maximum, s_cols), axis=-1, keepdims=True)
            if init:
                m_new = jnp.broadcast_to(m_blk, (tq - row0, LANES))
            else:
                m_old = m_ref[head, rows, :]
                m_new = jnp.maximum(m_old, m_blk)
            p = jnp.concatenate([jnp.exp2((sc - m_new).astype(BF16)) for sc in s_cols], axis=1)
            pv = _dot(p, v_ref[0, keys, ls])
            if init:
                acc_ref[head, rows, :] = pv
            else:
                acc_ref[head, rows, :] = jnp.exp2(m_old - m_new) * acc_ref[head, rows, :] + pv
            m_ref[head, rows, :] = m_new

    blocks_per_step = tq // TK
    for c in range(blocks_per_step):
        block(i * blocks_per_step + c, TK * c, True, init=(c == 0))

    def body(jj, carry):
        for c in range(blocks_per_step):
            block(jj * blocks_per_step + c, 0, False, init=False)
        return carry

    lax.fori_loop(0, i, body, 0)

    low = _low_half((tq, LANES))
    for pair in range(NUM_HEADS // 2):
        even = acc_ref[2 * pair]
        odd = acc_ref[2 * pair + 1]
        den = pltpu.roll(jnp.where(low, odd, even), HEAD_DIM, axis=1)
        o_ref[0, :, LANES * pair:LANES * (pair + 1)] = (jnp.where(low, even, odd) / den).astype(BF16)


def _attention(qe, qo, ke, ko, ve, vo):
    b, s, _ = qe.shape
    tq = min(TQ, s)
    assert tq % TK == 0 and s % tq == 0
    tri = jnp.where(lax.broadcasted_iota(jnp.int32, (TK, TK), 0)
                    >= lax.broadcasted_iota(jnp.int32, (TK, TK), 1), 0.0, MASK_VALUE).astype(F32)
    q_spec = pl.BlockSpec((1, tq, ATTN_WIDTH), lambda i, j: (i, j, 0))
    kv_spec = pl.BlockSpec((1, s, ATTN_WIDTH), lambda i, j: (i, 0, 0))
    return pl.pallas_call(
        _attn_kernel,
        grid=(b, s // tq),
        in_specs=[q_spec, q_spec, kv_spec, kv_spec, kv_spec, kv_spec, _const_spec(tri.shape)],
        out_specs=q_spec,
        out_shape=jax.ShapeDtypeStruct((b, s, ATTN_WIDTH), BF16),
        scratch_shapes=[pltpu.VMEM((NUM_HEADS, tq, LANES), F32),
                        pltpu.VMEM((NUM_HEADS, tq, LANES), F32)],
        compiler_params=pltpu.CompilerParams(
            dimension_semantics=("arbitrary", "arbitrary"), vmem_limit_bytes=VMEM_LIMIT_BYTES),
        name="attention",
    )(qe, qo, ke, ko, ve, vo, tri)


def _ssm_kernel(u_ref, un_ref, bw_ref, ar_ref, ai_ref, cw_ref, cb_ref, d_ref, wglu_ref, bglu_ref, o_ref,
                st0_ref, st1_ref, ub0_ref, ub1_ref, pin0_ref, pin1_ref, pout_ref, x_ref):
    nb, lt2, width = u_ref.shape
    lt = lt2 // 2
    npair = lt // 2
    prow = nb * npair
    n_set = width // LANES
    set_w = 2 * N_STATES // n_set
    half_set = set_w // 2

    def to_pairs(u, perm_ref):
        uf = u.astype(F32)
        for s in range(n_set):
            for b in range(nb):
                perm_ref[s, b * SSM_PITCH:b * SSM_PITCH + lt, :] = uf[b, :, LANES * s:LANES * (s + 1)]
        step = lambda t, s: perm_ref[s, pl.ds(t, nb, stride=SSM_PITCH), :]
        rows = [jnp.concatenate([step(2 * k + par, s) for s in range(n_set) for par in range(2)], axis=1)
                for k in range(npair)]
        return jnp.concatenate(rows, axis=0).astype(BF16)

    def store_seq_major(y, t0):
        for k in range(npair):
            for par in range(2):
                src = y[par * prow + nb * k:par * prow + nb * (k + 1)]
                for s in range(n_set):
                    pout_ref[s, pl.ds(2 * k + par, nb, stride=SSM_PITCH), :] = src[:, LANES * s:LANES * (s + 1)]
        for b in range(nb):
            o_ref[b, t0:t0 + lt, :] = jnp.concatenate(
                [pout_ref[s, b * SSM_PITCH:b * SSM_PITCH + lt, :] for s in range(n_set)],
                axis=1).astype(BF16)

    def project_in(ub_ref, st_ref):
        for j in range(n_set):
            st_ref[nb:nb + prow, set_w * j:set_w * (j + 1)] = _dot(
                ub_ref[:, 2 * LANES * j:2 * LANES * (j + 1)], bw_ref[j])

    def scan(st_ref):
        st_ref[0:nb, :] = x_ref[...]
        vregs_per_set = half_set // LANES
        for chunk in range(n_set * vregs_per_set // SCAN_SLABS):
            cols = []
            for v in range(chunk * SCAN_SLABS, (chunk + 1) * SCAN_SLABS):
                j, q = divmod(v, vregs_per_set)
                cols.append((set_w * j + LANES * q, set_w * j + half_set + LANES * q, LANES * v))
            ar = [jnp.broadcast_to(ar_ref[:, n:n + LANES], (nb, LANES)) for _, _, n in cols]
            ai = [jnp.broadcast_to(ai_ref[:, n:n + LANES], (nb, LANES)) for _, _, n in cols]
            xr = [st_ref[0:nb, rc:rc + LANES] for rc, _, _ in cols]
            xi = [st_ref[0:nb, ic:ic + LANES] for _, ic, _ in cols]
            for k in range(npair):
                rows = slice(nb * (k + 1), nb * (k + 2))
                for s, (rc, ic, _) in enumerate(cols):
                    r = ar[s] * xr[s] - ai[s] * xi[s] + st_ref[rows, rc:rc + LANES]
                    m = ar[s] * xi[s] + ai[s] * xr[s] + st_ref[rows, ic:ic + LANES]
                    st_ref[rows, rc:rc + LANES] = r
                    st_ref[rows, ic:ic + LANES] = m
                    xr[s], xi[s] = r, m
        x_ref[...] = st_ref[prow:prow + nb, :]

    def project_out(st_ref, ub_ref):
        y2 = [_dot(st_ref[:, set_w * j:set_w * (j + 1)].astype(BF16), cw_ref[j]) for j in range(n_set)]
        y_odd = jnp.concatenate([y[nb:, 0:LANES] for y in y2], axis=1)
        from_state = jnp.concatenate([y[:prow, LANES:2 * LANES] for y in y2], axis=1)
        u_even = jnp.concatenate(
            [ub_ref[:, 2 * LANES * s:2 * LANES * s + LANES] for s in range(n_set)], axis=1)
        u_odd = jnp.concatenate(
            [ub_ref[:, 2 * LANES * s + LANES:2 * LANES * (s + 1)] for s in range(n_set)], axis=1)
        half_w = width // 2
        direct = jnp.concatenate(
            [_dot(u_even[:, half_w * h:half_w * (h + 1)], cb_ref[h]) for h in range(2)], axis=1)
        y = jnp.concatenate([from_state + direct + d_ref[...] * u_even.astype(F32),
                             y_odd + d_ref[...] * u_odd.astype(F32)], axis=0)
        y = y / (1.0 + jnp.exp2(y * (GELU_K1 + GELU_K3 * (y * y))))
        return y * jax.nn.sigmoid(_dot(y.astype(BF16), wglu_ref[...]) + bglu_ref[...])

    @pl.when(pl.program_id(1) == 0)
    def _():
        x_ref[...] = jnp.zeros_like(x_ref)
        ub0_ref[...] = to_pairs(u_ref[:, 0:lt, :], pin0_ref)
        project_in(ub0_ref, st0_ref)

    ub1_ref[...] = to_pairs(u_ref[:, lt:lt2, :], pin1_ref)
    project_in(ub1_ref, st1_ref)
    scan(st0_ref)
    store_seq_major(project_out(st0_ref, ub0_ref), 0)
    ub0_ref[...] = to_pairs(un_ref[...], pin0_ref)
    project_in(ub0_ref, st0_ref)
    scan(st1_ref)
    store_seq_major(project_out(st1_ref, ub1_ref), lt)


def _ssm(u, layer, bw, ar, ai, cw, cb, d, wglu, bglu):
    b, s, _ = u.shape
    assert b % SSM_BATCH == 0 and s % (2 * LT) == 0 and LT % 2 == 0
    last_tile = s // LT - 1
    prow = SSM_BATCH * LT // 2
    tok = pl.BlockSpec((SSM_BATCH, 2 * LT, SSM_WIDTH), lambda i, j: (i, j, 0))
    nxt = pl.BlockSpec((SSM_BATCH, LT, SSM_WIDTH), lambda i, j: (i, jnp.minimum(2 * j + 2, last_tile), 0))
    state_buf = pltpu.VMEM((SSM_BATCH + prow, 2 * N_STATES), F32)
    u_buf = pltpu.VMEM((prow, 2 * SSM_WIDTH), BF16)
    perm_buf = pltpu.VMEM((SSM_WIDTH // LANES, SSM_BATCH * SSM_PITCH, LANES), F32)
    consts = (bw, ar, ai, cw, cb, d, wglu, bglu)
    return pl.pallas_call(
        _ssm_kernel,
        grid=(b // SSM_BATCH, s // (2 * LT)),
        in_specs=[tok, nxt] + [_const_spec(c.shape, layer) for c in consts],
        out_specs=tok,
        out_shape=jax.ShapeDtypeStruct(u.shape, BF16),
        scratch_shapes=[state_buf, state_buf, u_buf, u_buf, perm_buf, perm_buf, perm_buf,
                        pltpu.VMEM((SSM_BATCH, 2 * N_STATES), F32)],
        compiler_params=pltpu.CompilerParams(
            dimension_semantics=("arbitrary", "arbitrary"), vmem_limit_bytes=VMEM_LIMIT_BYTES),
        name="s5",
    )(u, u, *consts)


def _merge_mlp_kernel(x_ref, ya_ref, yb_ref, ga_ref, gb_ref, wa_ref, wb_ref, wo_ref, g_ref,
                      wup_ref, wdn_ref, gfin_ref, o_ref, *, final_norm):
    mixed = (ga_ref[...].astype(F32) * _dot(ya_ref[...], wa_ref[...])
             + gb_ref[...].astype(F32) * _dot(yb_ref[...], wb_ref[...]))
    x = x_ref[...] + _dot(mixed.astype(BF16), wo_ref[...])
    h = _rmsnorm(x, g_ref[...]).astype(BF16)
    for c in range(D_FF // FF_CHUNK):
        cs = slice(c * FF_CHUNK, (c + 1) * FF_CHUNK)
        up = jnp.maximum(_dot(h, wup_ref[:, cs]), 0.0)
        x = x + _dot((up * up).astype(BF16), wdn_ref[cs, :])
    if final_norm:
        x = _rmsnorm(x, gfin_ref[...])
    o_ref[...] = x


def _merge_mlp(x, ya, yb, ga, gb, layer, wa, wb, wo, g, wup, wdn, gfin, final_norm):
    t = x.shape[0]
    tm = min(TM_MLP, t)
    tok = lambda width: pl.BlockSpec((tm, width), lambda i: (i, 0))
    return pl.pallas_call(
        functools.partial(_merge_mlp_kernel, final_norm=final_norm),
        grid=(t // tm,),
        in_specs=[tok(D_MODEL), tok(ATTN_WIDTH), tok(SSM_WIDTH), tok(D_MODEL), tok(D_MODEL),
                  _const_spec(wa.shape, layer), _const_spec(wb.shape, layer),
                  _const_spec(wo.shape, layer), _const_spec(g.shape, layer),
                  _const_spec(wup.shape, layer), _const_spec(wdn.shape, layer),
                  _const_spec(gfin.shape)],
        out_specs=tok(D_MODEL),
        out_shape=jax.ShapeDtypeStruct(x.shape, F32),
        compiler_params=pltpu.CompilerParams(
            dimension_semantics=("arbitrary",), vmem_limit_bytes=VMEM_LIMIT_BYTES),
        name="merge_mlp",
    )(x, ya, yb, ga, gb, wa, wb, wo, g, wup, wdn, gfin)


def _ssm_params(lam_re, lam_im, log_dt, b_re, b_im, c_re, c_im):
    depth = lam_re.shape[0]
    dt = jnp.exp(log_dt)[..., None]
    mag = jnp.exp(lam_re * dt)
    ar = mag * jnp.cos(lam_im * dt)
    ai = mag * jnp.sin(lam_im * dt)
    den = lam_re * lam_re + lam_im * lam_im
    zr = ((ar - 1.0) * lam_re + ai * lam_im) / den
    zi = (ai * lam_re - (ar - 1.0) * lam_im) / den
    bb_re = zr[..., None] * b_re - zi[..., None] * b_im
    bb_im = zr[..., None] * b_im + zi[..., None] * b_re

    a2r = ar * ar - ai * ai
    a2i = 2.0 * ar * ai
    abb_re = ar[..., None] * bb_re - ai[..., None] * bb_im
    abb_im = ar[..., None] * bb_im + ai[..., None] * bb_re
    ca_re = c_re * ar[:, :, None, :] - c_im * ai[:, :, None, :]
    ca_im = c_re * ai[:, :, None, :] + c_im * ar[:, :, None, :]
    cb = (jnp.einsum("lgop,lgpi->lgio", c_re, bb_re) - jnp.einsum("lgop,lgpi->lgio", c_im, bb_im))

    n_set = SSM_WIDTH // LANES
    gs = SSM_GROUPS // n_set
    ch, ns = SSM_GROUP_CH, SSM_STATE

    def block_diag(quad, rows, cols, spec):
        m = jnp.stack([jnp.stack(pair) for pair in quad]).astype(BF16)
        m = m.reshape(2, 2, depth, n_set, gs, *quad[0][0].shape[2:])
        m = jnp.einsum(spec, m, jnp.eye(gs, dtype=BF16))
        m = m.reshape(2, 2, depth, n_set, gs * rows, gs * cols)
        return m.transpose(2, 3, 0, 4, 1, 5).reshape(depth, n_set, 2 * gs * rows, 2 * gs * cols)

    bw = block_diag([[abb_re, abb_im], [bb_re, bb_im]], ch, ns, "xylsgpc,gh->xylsgchp")
    cw = block_diag([[c_re, ca_re], [-c_im, -ca_im]], ns, ch, "xylsgcp,gh->xylsgphc")
    hg = SSM_GROUPS // 2
    cbw = jnp.einsum("lsgio,gh->lsgiho", cb.reshape(depth, 2, hg, ch, ch),
                     jnp.eye(hg, dtype=F32)).reshape(depth, 2, hg * ch, hg * ch)
    return (bw.astype(BF16), a2r.reshape(depth, 1, N_STATES), a2i.reshape(depth, 1, N_STATES),
            cw.astype(BF16), cbw.astype(BF16))


def _split_w_in(w_in, b_forget):
    depth = w_in.shape[0]
    o1 = ATTN_WIDTH
    o2 = o1 + ATTN_WIDTH
    o3 = o2 + ATTN_WIDTH
    o4 = o3 + NUM_HEADS
    o5 = o4 + SSM_WIDTH
    o6 = o5 + D_MODEL
    wq = (w_in[:, :, :o1] * (LOG2E * HEAD_DIM ** -0.5)).astype(BF16)
    reps = BIAS_PIECES * NUM_HEADS
    wf = jnp.zeros((depth, D_MODEL, LANES), F32).at[:, :, :reps].set(
        jnp.tile(w_in[:, :, o3:o4], (1, 1, BIAS_PIECES))).astype(BF16)
    bf = jnp.zeros((depth, 1, LANES), F32).at[:, 0, :reps].set(jnp.tile(b_forget, (1, BIAS_PIECES)))
    wb = w_in.astype(BF16)
    return (wq, wb[:, :, o1:o2], wb[:, :, o2:o3], wf, bf,
            wb[:, :, o4:o5], wb[:, :, o5:o6], wb[:, :, o6:])


@jax.jit
def kernel(x, norm_mix, w_in, b_forget, ssm_lambda_re, ssm_lambda_im, ssm_log_dt, ssm_b_re, ssm_b_im,
           ssm_c_re, ssm_c_im, ssm_d, w_glu, b_glu, w_branch_a, w_branch_b, w_out, norm_mlp,
           w_mlp_up, w_mlp_down, norm_final):
    b, s, d = x.shape
    depth = w_in.shape[0]
    t = b * s
    proj = (norm_mix.reshape(depth, 1, d),) + _split_w_in(w_in, b_forget)
    ssm = _ssm_params(ssm_lambda_re, ssm_lambda_im, ssm_log_dt, ssm_b_re, ssm_b_im, ssm_c_re, ssm_c_im)
    ssm = ssm + (ssm_d.reshape(depth, 1, SSM_WIDTH), w_glu.astype(BF16), b_glu.reshape(depth, 1, SSM_WIDTH))
    mlp = (w_branch_a.astype(BF16), w_branch_b.astype(BF16), w_out.astype(BF16),
           norm_mlp.reshape(depth, 1, d), w_mlp_up.astype(BF16), w_mlp_down.astype(BF16))
    gfin = norm_final.reshape(1, d)
    for l in range(depth):
        qe, qo, ke, ko, ve, vo, u, ga, gb = _inproj(x, l, *proj)
        ya = _attention(qe, qo, ke, ko, ve, vo)
        yb = _ssm(u, l, *ssm)
        x = _merge_mlp(x.reshape(t, d), ya.reshape(t, ATTN_WIDTH), yb.reshape(t, SSM_WIDTH),
                       ga.reshape(t, d), gb.reshape(t, d), l, *mlp, gfin,
                       final_norm=(l == depth - 1)).reshape(b, s, d)
    return x
```

```python
import functools
import math

import jax
import jax.numpy as jnp
from jax import lax
from jax.experimental import pallas as pl
from jax.experimental.pallas import tpu as pltpu

F32 = jnp.float32
BF16 = jnp.bfloat16

D_MODEL = 1024
NUM_HEADS = 8
HEAD_DIM = 64
ATTN_WIDTH = NUM_HEADS * HEAD_DIM
SSM_GROUPS = 32
SSM_GROUP_CH = 16
SSM_STATE = 64
SSM_WIDTH = SSM_GROUPS * SSM_GROUP_CH
N_STATES = SSM_GROUPS * SSM_STATE
D_FF = 4 * D_MODEL
RMS_EPS = 1e-6
MASK_VALUE = -1e30
LOG2E = math.log2(math.e)
BIAS_PIECES = 3
GELU_K1 = -2.0 * math.sqrt(2.0 / math.pi) * LOG2E
GELU_K3 = 0.044715 * GELU_K1

LANES = 128
SUBLANES = 8
VMEM_LIMIT_BYTES = 56 * 1024 * 1024

TM_PROJ = 1024
TQ = 1024
TK = 256
LT = 128
SSM_BATCH = SUBLANES
SSM_PITCH = LT + SUBLANES
SCAN_SLABS = 16
TM_MLP = 512
FF_CHUNK = 1024

_dot = functools.partial(jnp.dot, preferred_element_type=F32)


def _rmsnorm(x, g):
    ms = jnp.mean(x * x, axis=-1, keepdims=True)
    return x * lax.rsqrt(ms + RMS_EPS) * g


def _const_spec(shape, layer=None):
    if layer is None:
        zeros = (0,) * len(shape)
        return pl.BlockSpec(shape, lambda *_: zeros, pipeline_mode=pl.Buffered(1))
    index = (layer,) + (0,) * (len(shape) - 1)
    return pl.BlockSpec((None,) + tuple(shape[1:]), lambda *_: index, pipeline_mode=pl.Buffered(1))


def _low_half(shape):
    lane = lax.broadcasted_iota(jnp.int32, shape, len(shape) - 1)
    return (lane & (LANES - 1)) < HEAD_DIM


def _cumsum_rows(x):
    n = x.shape[0]
    row = lax.broadcasted_iota(jnp.int32, x.shape, 0)
    k = 1
    while k < n:
        x = x + jnp.where(row >= k, pltpu.roll(x, k, axis=0), 0.0)
        k *= 2
    return x


def _inproj_kernel(x_ref, g_ref, wq_ref, wk_ref, wv_ref, wf_ref, bf_ref, pe_ref, po_ref, one_ref,
                   wu_ref, wga_ref, wgb_ref,
                   qe_ref, qo_ref, ke_ref, ko_ref, ve_ref, vo_ref, u_ref, ga_ref, gb_ref, carry_ref):
    @pl.when(pl.program_id(1) == 0)
    def _():
        carry_ref[...] = jnp.zeros_like(carry_ref)

    tm = x_ref.shape[1]
    h = _rmsnorm(x_ref[0], g_ref[...]).astype(BF16)
    low = _low_half((tm, ATTN_WIDTH))
    q_ones = one_ref[0:1, :]
    q_ones_odd = one_ref[1:2, :]

    q = _dot(h, wq_ref[...])
    qe_ref[0] = jnp.where(low, q, q_ones).astype(BF16)
    qo_ref[0] = jnp.where(low, q_ones_odd, q).astype(BF16)
    v = _dot(h, wv_ref[...])
    ve_ref[0] = jnp.where(low, v, 1.0).astype(BF16)
    vo_ref[0] = jnp.where(low, 1.0, v).astype(BF16)

    fl = _dot(h, wf_ref[...]) + bf_ref[...]
    log_f = jnp.minimum(fl, 0.0) - jnp.log1p(jnp.exp(-jnp.abs(fl)))
    cum = _cumsum_rows(log_f) + carry_ref[0:1, :]
    carry_ref[...] = jnp.broadcast_to(cum[tm - 1:tm, :], carry_ref.shape)
    bias = cum * (-LOG2E)
    hi = bias.astype(BF16).astype(F32)
    mid = (bias - hi).astype(BF16).astype(F32)
    lo = (bias - hi - mid).astype(BF16).astype(F32)
    lane = lax.broadcasted_iota(jnp.int32, bias.shape, 1)
    pieces = jnp.where(lane < NUM_HEADS, hi, jnp.where(lane < 2 * NUM_HEADS, mid, lo)).astype(BF16)
    k = _dot(h, wk_ref[...])
    ke_ref[0] = jnp.where(low, k, _dot(pieces, pe_ref[...])).astype(BF16)
    ko_ref[0] = jnp.where(low, _dot(pieces, po_ref[...]), k).astype(BF16)

    u_ref[0] = _dot(h, wu_ref[...]).astype(BF16)
    ga_ref[0] = jax.nn.sigmoid(_dot(h, wga_ref[...])).astype(BF16)
    gb_ref[0] = jax.nn.sigmoid(_dot(h, wgb_ref[...])).astype(BF16)


def _attn_constants():
    pe = [[0.0] * ATTN_WIDTH for _ in range(LANES)]
    po = [[0.0] * ATTN_WIDTH for _ in range(LANES)]
    ones = [[0.0] * ATTN_WIDTH for _ in range(SUBLANES)]
    for pair in range(NUM_HEADS // 2):
        for i in range(BIAS_PIECES):
            pe[2 * pair + NUM_HEADS * i][LANES * pair + HEAD_DIM + i] = 1.0
            po[2 * pair + 1 + NUM_HEADS * i][LANES * pair + i] = 1.0
            ones[0][LANES * pair + HEAD_DIM + i] = 1.0
            ones[1][LANES * pair + i] = 1.0
    return jnp.array(pe, BF16), jnp.array(po, BF16), jnp.array(ones, F32)


def _inproj(x, layer, g, wq, wk, wv, wf, bf, wu, wga, wgb):
    b, s, _ = x.shape
    tm = min(TM_PROJ, s)
    pe, po, ones = _attn_constants()
    tok = lambda width: pl.BlockSpec((1, tm, width), lambda i, j: (i, j, 0))
    act = lambda width: jax.ShapeDtypeStruct((b, s, width), BF16)
    consts = (g, wq, wk, wv, wf, bf, pe, po, ones, wu, wga, wgb)
    shared = (pe, po, ones)
    return pl.pallas_call(
        _inproj_kernel,
        grid=(b, s // tm),
        in_specs=[tok(D_MODEL)] + [
            _const_spec(c.shape, None if any(c is t for t in shared) else layer) for c in consts],
        out_specs=[tok(ATTN_WIDTH)] * 6 + [tok(SSM_WIDTH), tok(D_MODEL), tok(D_MODEL)],
        out_shape=[act(ATTN_WIDTH)] * 6 + [act(SSM_WIDTH), act(D_MODEL), act(D_MODEL)],
        scratch_shapes=[pltpu.VMEM((SUBLANES, LANES), F32)],
        compiler_params=pltpu.CompilerParams(
            dimension_semantics=("arbitrary", "arbitrary"), vmem_limit_bytes=VMEM_LIMIT_BYTES),
        name="inproj",
    )(x, *consts)


def _attn_kernel(qe_ref, qo_ref, ke_ref, ko_ref, ve_ref, vo_ref, tri_ref, o_ref, m_ref, acc_ref):
    i = pl.program_id(1)
    tq = o_ref.shape[1]

    def block(j, row0, diagonal, init):
        keys = pl.ds(pl.multiple_of(j * TK, TK), TK)
        rows = slice(row0, tq)
        for head in range(NUM_HEADS):
            ls = slice(LANES * (head // 2), LANES * (head // 2 + 1))
            q_ref, k_ref, v_ref = (qe_ref, ke_ref, ve_ref) if head % 2 == 0 else (qo_ref, ko_ref, vo_ref)
            s = lax.dot_general(q_ref[0, rows, ls], k_ref[0, keys, ls], (((1,), (1,)), ((), ())),
                                preferred_element_type=F32)
            if diagonal:
                top = s[:TK] + tri_ref[...]
                s = top if tq - row0 == TK else jnp.concatenate([top, s[TK:]], axis=0)
            s_cols = [s[:, LANES * c:LANES * (c + 1)] for c in range(TK // LANES)]
            m_blk = jnp.max(functools.reduce(jnp.maximum, s_cols), axis=-1, keepdims=True)
            if init:
                m_new = jnp.broadcast_to(m_blk, (tq - row0, LANES))
            else:
                m_old = m_ref[head, rows, :]
                m_new = jnp.maximum(m_old, m_blk)
            p = jnp.concatenate([jnp.exp2((sc - m_new).astype(BF16)) for sc in s_cols], axis=1)
            pv = _dot(p, v_ref[0, keys, ls])
            if init:
                acc_ref[head, rows, :] = pv
            else:
                acc_ref[head, rows, :] = jnp.exp2(m_old - m_new) * acc_ref[head, rows, :] + pv
            m_ref[head, rows, :] = m_new

    blocks_per_step = tq // TK
    for c in range(blocks_per_step):
        block(i * blocks_per_step + c, TK * c, True, init=(c == 0))

    def body(jj, carry):
        for c in range(blocks_per_step):
            block(jj * blocks_per_step + c, 0, False, init=False)
        return carry

    lax.fori_loop(0, i, body, 0)

    low = _low_half((tq, LANES))
    for pair in range(NUM_HEADS // 2):
        even = acc_ref[2 * pair]
        odd = acc_ref[2 * pair + 1]
        den = pltpu.roll(jnp.where(low, odd, even), HEAD_DIM, axis=1)
        o_ref[0, :, LANES * pair:LANES * (pair + 1)] = (jnp.where(low, even, odd) / den).astype(BF16)


def _attention(qe, qo, ke, ko, ve, vo):
    b, s, _ = qe.shape
    tq = min(TQ, s)
    assert tq % TK == 0 and s % tq == 0
    tri = jnp.where(lax.broadcasted_iota(jnp.int32, (TK, TK), 0)
                    >= lax.broadcasted_iota(jnp.int32, (TK, TK), 1), 0.0, MASK_VALUE).astype(F32)
    q_spec = pl.BlockSpec((1, tq, ATTN_WIDTH), lambda i, j: (i, j, 0))
    kv_spec = pl.BlockSpec((1, s, ATTN_WIDTH), lambda i, j: (i, 0, 0))
    return pl.pallas_call(
        _attn_kernel,
        grid=(b, s // tq),
        in_specs=[q_spec, q_spec, kv_spec, kv_spec, kv_spec, kv_spec, _const_spec(tri.shape)],
        out_specs=q_spec,
        out_shape=jax.ShapeDtypeStruct((b, s, ATTN_WIDTH), BF16),
        scratch_shapes=[pltpu.VMEM((NUM_HEADS, tq, LANES), F32),
                        pltpu.VMEM((NUM_HEADS, tq, LANES), F32)],
        compiler_params=pltpu.CompilerParams(
            dimension_semantics=("arbitrary", "arbitrary"), vmem_limit_bytes=VMEM_LIMIT_BYTES),
        name="attention",
    )(qe, qo, ke, ko, ve, vo, tri)


def _ssm_kernel(u_ref, un_ref, bw_ref, ar_ref, ai_ref, cw_ref, cb_ref, d_ref, wglu_ref, bglu_ref, o_ref,
                st0_ref, st1_ref, ub0_ref, ub1_ref, pin0_ref, pin1_ref, pout_ref, x_ref):
    nb, lt2, width = u_ref.shape
    lt = lt2 // 2
    npair = lt // 2
    prow = nb * npair
    n_set = width // LANES
    set_w = 2 * N_STATES // n_set
    half_set = set_w // 2

    def to_pairs(u, perm_ref):
        uf = u.astype(F32)
        for s in range(n_set):
            for b in range(nb):
                perm_ref[s, b * SSM_PITCH:b * SSM_PITCH + lt, :] = uf[b, :, LANES * s:LANES * (s + 1)]
        step = lambda t, s: perm_ref[s, pl.ds(t, nb, stride=SSM_PITCH), :]
        rows = [jnp.concatenate([step(2 * k + par, s) for s in range(n_set) for par in range(2)], axis=1)
                for k in range(npair)]
        return jnp.concatenate(rows, axis=0).astype(BF16)

    def store_seq_major(y, t0):
        for k in range(npair):
            for par in range(2):
                src = y[par * prow + nb * k:par * prow + nb * (k + 1)]
                for s in range(n_set):
                    pout_ref[s, pl.ds(2 * k + par, nb, stride=SSM_PITCH), :] = src[:, LANES * s:LANES * (s + 1)]
        for b in range(nb):
            o_ref[b, t0:t0 + lt, :] = jnp.concatenate(
                [pout_ref[s, b * SSM_PITCH:b * SSM_PITCH + lt, :] for s in range(n_set)],
                axis=1).astype(BF16)

    def project_in(ub_ref, st_ref):
        for j in range(n_set):
            st_ref[nb:nb + prow, set_w * j:set_w * (j + 1)] = _dot(
                ub_ref[:, 2 * LANES * j:2 * LANES * (j + 1)], bw_ref[j])

    def scan(st_ref):
        st_ref[0:nb, :] = x_ref[...]
        vregs_per_set = half_set // LANES
        for chunk in range(n_set * vregs_per_set // SCAN_SLABS):
            cols = []
            for v in range(chunk * SCAN_SLABS, (chunk + 1) * SCAN_SLABS):
                j, q = divmod(v, vregs_per_set)
                cols.append((set_w * j + LANES * q, set_w * j + half_set + LANES * q, LANES * v))
            ar = [jnp.broadcast_to(ar_ref[:, n:n + LANES], (nb, LANES)) for _, _, n in cols]
            ai = [jnp.broadcast_to(ai_ref[:, n:n + LANES], (nb, LANES)) for _, _, n in cols]
            xr = [st_ref[0:nb, rc:rc + LANES] for rc, _, _ in cols]
            xi = [st_ref[0:nb, ic:ic + LANES] for _, ic, _ in cols]
            for k in range(npair):
                rows = slice(nb * (k + 1), nb * (k + 2))
                for s, (rc, ic, _) in enumerate(cols):
                    r = ar[s] * xr[s] - ai[s] * xi[s] + st_ref[rows, rc:rc + LANES]
                    m = ar[s] * xi[s] + ai[s] * xr[s] + st_ref[rows, ic:ic + LANES]
                    st_ref[rows, rc:rc + LANES] = r
                    st_ref[rows, ic:ic + LANES] = m
                    xr[s], xi[s] = r, m
        x_ref[...] = st_ref[prow:prow + nb, :]

    def project_out(st_ref, ub_ref):
        y2 = [_dot(st_ref[:, set_w * j:set_w * (j + 1)].astype(BF16), cw_ref[j]) for j in range(n_set)]
        y_odd = jnp.concatenate([y[nb:, 0:LANES] for y in y2], axis=1)
        from_state = jnp.concatenate([y[:prow, LANES:2 * LANES] for y in y2], axis=1)
        u_even = jnp.concatenate(
            [ub_ref[:, 2 * LANES * s:2 * LANES * s + LANES] for s in range(n_set)], axis=1)
        u_odd = jnp.concatenate(
            [ub_ref[:, 2 * LANES * s + LANES:2 * LANES * (s + 1)] for s in range(n_set)], axis=1)
        half_w = width // 2
        direct = jnp.concatenate(
            [_dot(u_even[:, half_w * h:half_w * (h + 1)], cb_ref[h]) for h in range(2)], axis=1)
        y = jnp.concatenate([from_state + direct + d_ref[...] * u_even.astype(F32),
                             y_odd + d_ref[...] * u_odd.astype(F32)], axis=0)
        y = y / (1.0 + jnp.exp2(y * (GELU_K1 + GELU_K3 * (y * y))))
        return y * jax.nn.sigmoid(_dot(y.astype(BF16), wglu_ref[...]) + bglu_ref[...])

    @pl.when(pl.program_id(1) == 0)
    def _():
        x_ref[...] = jnp.zeros_like(x_ref)
        ub0_ref[...] = to_pairs(u_ref[:, 0:lt, :], pin0_ref)
        project_in(ub0_ref, st0_ref)

    ub1_ref[...] = to_pairs(u_ref[:, lt:lt2, :], pin1_ref)
    project_in(ub1_ref, st1_ref)
    scan(st0_ref)
    store_seq_major(project_out(st0_ref, ub0_ref), 0)
    ub0_ref[...] = to_pairs(un_ref[...], pin0_ref)
    project_in(ub0_ref, st0_ref)
    scan(st1_ref)
    store_seq_major(project_out(st1_ref, ub1_ref), lt)


def _ssm(u, layer, bw, ar, ai, cw, cb, d, wglu, bglu):
    b, s, _ = u.shape
    assert b % SSM_BATCH == 0 and s % (2 * LT) == 0 and LT % 2 == 0
    last_tile = s // LT - 1
    prow = SSM_BATCH * LT // 2
    tok = pl.BlockSpec((SSM_BATCH, 2 * LT, SSM_WIDTH), lambda i, j: (i, j, 0))
    nxt = pl.BlockSpec((SSM_BATCH, LT, SSM_WIDTH), lambda i, j: (i, jnp.minimum(2 * j + 2, last_tile), 0))
    state_buf = pltpu.VMEM((SSM_BATCH + prow, 2 * N_STATES), F32)
    u_buf = pltpu.VMEM((prow, 2 * SSM_WIDTH), BF16)
    perm_buf = pltpu.VMEM((SSM_WIDTH // LANES, SSM_BATCH * SSM_PITCH, LANES), F32)
    consts = (bw, ar, ai, cw, cb, d, wglu, bglu)
    return pl.pallas_call(
        _ssm_kernel,
        grid=(b // SSM_BATCH, s // (2 * LT)),
        in_specs=[tok, nxt] + [_const_spec(c.shape, layer) for c in consts],
        out_specs=tok,
        out_shape=jax.ShapeDtypeStruct(u.shape, BF16),
        scratch_shapes=[state_buf, state_buf, u_buf, u_buf, perm_buf, perm_buf, perm_buf,
                        pltpu.VMEM((SSM_BATCH, 2 * N_STATES), F32)],
        compiler_params=pltpu.CompilerParams(
            dimension_semantics=("arbitrary", "arbitrary"), vmem_limit_bytes=VMEM_LIMIT_BYTES),
        name="s5",
    )(u, u, *consts)


def _merge_mlp_kernel(x_ref, ya_ref, yb_ref, ga_ref, gb_ref, wa_ref, wb_ref, wo_ref, g_ref,
                      wup_ref, wdn_ref, gfin_ref, o_ref, *, final_norm):
    mixed = (ga_ref[...].astype(F32) * _dot(ya_ref[...], wa_ref[...])
             + gb_ref[...].astype(F32) * _dot(yb_ref[...], wb_ref[...]))
    x = x_ref[...] + _dot(mixed.astype(BF16), wo_ref[...])
    h = _rmsnorm(x, g_ref[...]).astype(BF16)
    for c in range(D_FF // FF_CHUNK):
        cs = slice(c * FF_CHUNK, (c + 1) * FF_CHUNK)
        up = jnp.maximum(_dot(h, wup_ref[:, cs]), 0.0)
        x = x + _dot((up * up).astype(BF16), wdn_ref[cs, :])
    if final_norm:
        x = _rmsnorm(x, gfin_ref[...])
    o_ref[...] = x


def _merge_mlp(x, ya, yb, ga, gb, layer, wa, wb, wo, g, wup, wdn, gfin, final_norm):
    t = x.shape[0]
    tm = min(TM_MLP, t)
    tok = lambda width: pl.BlockSpec((tm, width), lambda i: (i, 0))
    return pl.pallas_call(
        functools.partial(_merge_mlp_kernel, final_norm=final_norm),
        grid=(t // tm,),
        in_specs=[tok(D_MODEL), tok(ATTN_WIDTH), tok(SSM_WIDTH), tok(D_MODEL), tok(D_MODEL),
                  _const_spec(wa.shape, layer), _const_spec(wb.shape, layer),
                  _const_spec(wo.shape, layer), _const_spec(g.shape, layer),
                  _const_spec(wup.shape, layer), _const_spec(wdn.shape, layer),
                  _const_spec(gfin.shape)],
        out_specs=tok(D_MODEL),
        out_shape=jax.ShapeDtypeStruct(x.shape, F32),
        compiler_params=pltpu.CompilerParams(
            dimension_semantics=("arbitrary",), vmem_limit_bytes=VMEM_LIMIT_BYTES),
        name="merge_mlp",
    )(x, ya, yb, ga, gb, wa, wb, wo, g, wup, wdn, gfin)


def _ssm_params(lam_re, lam_im, log_dt, b_re, b_im, c_re, c_im):
    depth = lam_re.shape[0]
    dt = jnp.exp(log_dt)[..., None]
    mag = jnp.exp(lam_re * dt)
    ar = mag * jnp.cos(lam_im * dt)
    ai = mag * jnp.sin(lam_im * dt)
    den = lam_re * lam_re + lam_im * lam_im
    zr = ((ar - 1.0) * lam_re + ai * lam_im) / den
    zi = (ai * lam_re - (ar - 1.0) * lam_im) / den
    bb_re = zr[..., None] * b_re - zi[..., None] * b_im
    bb_im = zr[..., None] * b_im + zi[..., None] * b_re

    a2r = ar * ar - ai * ai
    a2i = 2.0 * ar * ai
    abb_re = ar[..., None] * bb_re - ai[..., None] * bb_im
    abb_im = ar[..., None] * bb_im + ai[..., None] * bb_re
    ca_re = c_re * ar[:, :, None, :] - c_im * ai[:, :, None, :]
    ca_im = c_re * ai[:, :, None, :] + c_im * ar[:, :, None, :]
    cb = (jnp.einsum("lgop,lgpi->lgio", c_re, bb_re) - jnp.einsum("lgop,lgpi->lgio", c_im, bb_im))

    n_set = SSM_WIDTH // LANES
    gs = SSM_GROUPS // n_set
    ch, ns = SSM_GROUP_CH, SSM_STATE

    def block_diag(quad, rows, cols):
        m = jnp.stack([jnp.stack(pair) for pair in quad]).astype(BF16)
        m = m.reshape(2, 2, depth, n_set, gs, cols, rows).transpose(2, 3, 0, 4, 6, 1, 5)
        m = jnp.einsum("lsrgicj,gh->lsrgichj", m, jnp.eye(gs, dtype=BF16))
        return m.reshape(depth, n_set, 2 * gs * rows, 2 * gs * cols)

    bw = block_diag([[abb_re, abb_im], [bb_re, bb_im]], ch, ns)
    cw = block_diag([[c_re, ca_re], [-c_im, -ca_im]], ns, ch)
    hg = SSM_GROUPS // 2
    cbw = jnp.einsum("lsgio,gh->lsgiho", cb.reshape(depth, 2, hg, ch, ch),
                     jnp.eye(hg, dtype=F32)).reshape(depth, 2, hg * ch, hg * ch)
    return (bw.astype(BF16), a2r.reshape(depth, 1, N_STATES), a2i.reshape(depth, 1, N_STATES),
            cw.astype(BF16), cbw.astype(BF16))


def _split_w_in(w_in, b_forget):
    depth = w_in.shape[0]
    o1 = ATTN_WIDTH
    o2 = o1 + ATTN_WIDTH
    o3 = o2 + ATTN_WIDTH
    o4 = o3 + NUM_HEADS
    o5 = o4 + SSM_WIDTH
    o6 = o5 + D_MODEL
    wq = (w_in[:, :, :o1] * (LOG2E * HEAD_DIM ** -0.5)).astype(BF16)
    reps = BIAS_PIECES * NUM_HEADS
    wf = jnp.zeros((depth, D_MODEL, LANES), F32).at[:, :, :reps].set(
        jnp.tile(w_in[:, :, o3:o4], (1, 1, BIAS_PIECES))).astype(BF16)
    bf = jnp.zeros((depth, 1, LANES), F32).at[:, 0, :reps].set(jnp.tile(b_forget, (1, BIAS_PIECES)))
    wb = w_in.astype(BF16)
    return (wq, wb[:, :, o1:o2], wb[:, :, o2:o3], wf, bf,
            wb[:, :, o4:o5], wb[:, :, o5:o6], wb[:, :, o6:])


@jax.jit
def kernel(x, norm_mix, w_in, b_forget, ssm_lambda_re, ssm_lambda_im, ssm_log_dt, ssm_b_re, ssm_b_im,
           ssm_c_re, ssm_c_im, ssm_d, w_glu, b_glu, w_branch_a, w_branch_b, w_out, norm_mlp,
           w_mlp_up, w_mlp_down, norm_final):
    b, s, d = x.shape
    depth = w_in.shape[0]
    t = b * s
    proj = (norm_mix.reshape(depth, 1, d),) + _split_w_in(w_in, b_forget)
    ssm = _ssm_params(ssm_lambda_re, ssm_lambda_im, ssm_log_dt, ssm_b_re, ssm_b_im, ssm_c_re, ssm_c_im)
    ssm = ssm + (ssm_d.reshape(depth, 1, SSM_WIDTH), w_glu.astype(BF16), b_glu.reshape(depth, 1, SSM_WIDTH))
    mlp = (w_branch_a.astype(BF16), w_branch_b.astype(BF16), w_out.astype(BF16),
           norm_mlp.reshape(depth, 1, d), w_mlp_up.astype(BF16), w_mlp_down.astype(BF16))
    gfin = norm_final.reshape(1, d)
    for l in range(depth):
        qe, qo, ke, ko, ve, vo, u, ga, gb = _inproj(x, l, *proj)
        ya = _attention(qe, qo, ke, ko, ve, vo)
        yb = _ssm(u, l, *ssm)
        x = _merge_mlp(x.reshape(t, d), ya.reshape(t, ATTN_WIDTH), yb.reshape(t, SSM_WIDTH),
                       ga.reshape(t, d), gb.reshape(t, d), l, *mlp, gfin,
                       final_norm=(l == depth - 1)).reshape(b, s, d)
    return x
```

```python
import functools
import math

import jax
import jax.numpy as jnp
from jax import lax
from jax.experimental import pallas as pl
from jax.experimental.pallas import tpu as pltpu

F32 = jnp.float32
BF16 = jnp.bfloat16

D_MODEL = 1024
NUM_HEADS = 8
HEAD_DIM = 64
ATTN_WIDTH = NUM_HEADS * HEAD_DIM
SSM_GROUPS = 32
SSM_GROUP_CH = 16
SSM_STATE = 64
SSM_WIDTH = SSM_GROUPS * SSM_GROUP_CH
N_STATES = SSM_GROUPS * SSM_STATE
D_FF = 4 * D_MODEL
RMS_EPS = 1e-6
MASK_VALUE = -1e30
LOG2E = math.log2(math.e)
GELU_K1 = -2.0 * math.sqrt(2.0 / math.pi) * LOG2E
GELU_K3 = 0.044715 * GELU_K1

LANES = 128
SUBLANES = 8
VMEM_LIMIT_BYTES = 56 * 1024 * 1024

TM_PROJ = 1024
TQ = 1024
TK = 256
LT = 128
SSM_BATCH = SUBLANES
SSM_PITCH = LT + SUBLANES
SCAN_SLABS = 16
TM_MLP = 512
FF_CHUNK = 1024

_dot = functools.partial(jnp.dot, preferred_element_type=F32)


def _rmsnorm(x, g):
    ms = jnp.mean(x * x, axis=-1, keepdims=True)
    return x * lax.rsqrt(ms + RMS_EPS) * g


def _const_spec(shape, layer=None):
    if layer is None:
        zeros = (0,) * len(shape)
        return pl.BlockSpec(shape, lambda *_: zeros, pipeline_mode=pl.Buffered(1))
    index = (layer,) + (0,) * (len(shape) - 1)
    return pl.BlockSpec((None,) + tuple(shape[1:]), lambda *_: index, pipeline_mode=pl.Buffered(1))


def _low_half(shape):
    lane = lax.broadcasted_iota(jnp.int32, shape, len(shape) - 1)
    return (lane & (LANES - 1)) < HEAD_DIM


def _cumsum_rows(x):
    n = x.shape[0]
    row = lax.broadcasted_iota(jnp.int32, x.shape, 0)
    k = 1
    while k < n:
        x = x + jnp.where(row >= k, pltpu.roll(x, k, axis=0), 0.0)
        k *= 2
    return x


def _inproj_kernel(x_ref, g_ref, wq_ref, wk_ref, wv_ref, wf_ref, bf_ref,
                   wu_ref, wga_ref, wgb_ref,
                   qe_ref, qo_ref, ke_ref, ko_ref, ve_ref, vo_ref, bias_ref, u_ref, ga_ref, gb_ref,
                   carry_ref):
    @pl.when(pl.program_id(1) == 0)
    def _():
        carry_ref[...] = jnp.zeros_like(carry_ref)

    tm = x_ref.shape[1]
    h = _rmsnorm(x_ref[0], g_ref[...]).astype(BF16)
    low = _low_half((tm, ATTN_WIDTH))

    q = _dot(h, wq_ref[...])
    qe_ref[0] = jnp.where(low, q, 0.0).astype(BF16)
    qo_ref[0] = jnp.where(low, 0.0, q).astype(BF16)
    k = _dot(h, wk_ref[...])
    ke_ref[0] = jnp.where(low, k, 0.0).astype(BF16)
    ko_ref[0] = jnp.where(low, 0.0, k).astype(BF16)
    v = _dot(h, wv_ref[...])
    ve_ref[0] = jnp.where(low, v, 1.0).astype(BF16)
    vo_ref[0] = jnp.where(low, 1.0, v).astype(BF16)

    fl = _dot(h, wf_ref[...]) + bf_ref[...]
    log_f = jnp.minimum(fl, 0.0) - jnp.log1p(jnp.exp(-jnp.abs(fl)))
    cum = _cumsum_rows(log_f) + carry_ref[0:1, :]
    carry_ref[...] = jnp.broadcast_to(cum[tm - 1:tm, :], carry_ref.shape)
    bias = (cum * (-LOG2E)).T[:NUM_HEADS]
    for j in range(tm // TK):
        bias_ref[0, j] = bias[:, j * TK:(j + 1) * TK]

    u_ref[0] = _dot(h, wu_ref[...]).astype(BF16)
    ga_ref[0] = jax.nn.sigmoid(_dot(h, wga_ref[...])).astype(BF16)
    gb_ref[0] = jax.nn.sigmoid(_dot(h, wgb_ref[...])).astype(BF16)


def _inproj(x, layer, g, wq, wk, wv, wf, bf, wu, wga, wgb):
    b, s, _ = x.shape
    tm = min(TM_PROJ, s)
    assert tm % TK == 0
    tok = lambda width: pl.BlockSpec((1, tm, width), lambda i, j: (i, j, 0))
    act = lambda width: jax.ShapeDtypeStruct((b, s, width), BF16)
    consts = (g, wq, wk, wv, wf, bf, wu, wga, wgb)
    bias_spec = pl.BlockSpec((1, tm // TK, NUM_HEADS, TK), lambda i, j: (i, j, 0, 0))
    bias_shape = jax.ShapeDtypeStruct((b, s // TK, NUM_HEADS, TK), F32)
    return pl.pallas_call(
        _inproj_kernel,
        grid=(b, s // tm),
        in_specs=[tok(D_MODEL)] + [_const_spec(c.shape, layer) for c in consts],
        out_specs=[tok(ATTN_WIDTH)] * 6 + [bias_spec, tok(SSM_WIDTH), tok(D_MODEL), tok(D_MODEL)],
        out_shape=[act(ATTN_WIDTH)] * 6 + [bias_shape, act(SSM_WIDTH), act(D_MODEL), act(D_MODEL)],
        scratch_shapes=[pltpu.VMEM((SUBLANES, LANES), F32)],
        compiler_params=pltpu.CompilerParams(
            dimension_semantics=("arbitrary", "arbitrary"), vmem_limit_bytes=VMEM_LIMIT_BYTES),
        name="inproj",
    )(x, *consts)


def _attn_kernel(qe_ref, qo_ref, ke_ref, ko_ref, ve_ref, vo_ref, bias_ref, tri_ref, o_ref, m_ref, acc_ref):
    i = pl.program_id(1)
    tq = o_ref.shape[1]

    def block(j, row0, diagonal, init):
        keys = pl.ds(pl.multiple_of(j * TK, TK), TK)
        rows = slice(row0, tq)
        for head in range(NUM_HEADS):
            ls = slice(LANES * (head // 2), LANES * (head // 2 + 1))
            q_ref, k_ref, v_ref = (qe_ref, ke_ref, ve_ref) if head % 2 == 0 else (qo_ref, ko_ref, vo_ref)
            s = lax.dot_general(q_ref[0, rows, ls], k_ref[0, keys, ls], (((1,), (1,)), ((), ())),
                                preferred_element_type=F32)
            s = s + bias_ref[0, j, head:head + 1, :]
            if diagonal:
                top = s[:TK] + tri_ref[...]
                s = top if tq - row0 == TK else jnp.concatenate([top, s[TK:]], axis=0)
            s_cols = [s[:, LANES * c:LANES * (c + 1)] for c in range(TK // LANES)]
            m_blk = jnp.max(functools.reduce(jnp.maximum, s_cols), axis=-1, keepdims=True)
            if init:
                m_new = jnp.broadcast_to(m_blk, (tq - row0, LANES))
            else:
                m_old = m_ref[head, rows, :]
                m_new = jnp.maximum(m_old, m_blk)
            p = jnp.concatenate([jnp.exp2((sc - m_new).astype(BF16)) for sc in s_cols], axis=1)
            pv = _dot(p, v_ref[0, keys, ls])
            if init:
                acc_ref[head, rows, :] = pv
            else:
                acc_ref[head, rows, :] = jnp.exp2(m_old - m_new) * acc_ref[head, rows, :] + pv
            m_ref[head, rows, :] = m_new

    blocks_per_step = tq // TK
    for c in range(blocks_per_step):
        block(i * blocks_per_step + c, TK * c, True, init=(c == 0))

    def body(jj, carry):
        for c in range(blocks_per_step):
            block(jj * blocks_per_step + c, 0, False, init=False)
        return carry

    lax.fori_loop(0, i, body, 0)

    low = _low_half((tq, LANES))
    for pair in range(NUM_HEADS // 2):
        even = acc_ref[2 * pair]
        odd = acc_ref[2 * pair + 1]
        den = pltpu.roll(jnp.where(low, odd, even), HEAD_DIM, axis=1)
        o_ref[0, :, LANES * pair:LANES * (pair + 1)] = (jnp.where(low, even, odd) / den).astype(BF16)


def _attention(qe, qo, ke, ko, ve, vo, bias):
    b, s, _ = qe.shape
    tq = min(TQ, s)
    assert tq % TK == 0 and s % tq == 0
    tri = jnp.where(lax.broadcasted_iota(jnp.int32, (TK, TK), 0)
                    >= lax.broadcasted_iota(jnp.int32, (TK, TK), 1), 0.0, MASK_VALUE).astype(F32)
    q_spec = pl.BlockSpec((1, tq, ATTN_WIDTH), lambda i, j: (i, j, 0))
    kv_spec = pl.BlockSpec((1, s, ATTN_WIDTH), lambda i, j: (i, 0, 0))
    return pl.pallas_call(
        _attn_kernel,
        grid=(b, s // tq),
        in_specs=[q_spec, q_spec, kv_spec, kv_spec, kv_spec, kv_spec,
                  pl.BlockSpec((1, s // TK, NUM_HEADS, TK), lambda i, j: (i, 0, 0, 0)),
                  _const_spec(tri.shape)],
        out_specs=q_spec,
        out_shape=jax.ShapeDtypeStruct((b, s, ATTN_WIDTH), BF16),
        scratch_shapes=[pltpu.VMEM((NUM_HEADS, tq, LANES), F32),
                        pltpu.VMEM((NUM_HEADS, tq, LANES), F32)],
        compiler_params=pltpu.CompilerParams(
            dimension_semantics=("arbitrary", "arbitrary"), vmem_limit_bytes=VMEM_LIMIT_BYTES),
        name="attention",
    )(qe, qo, ke, ko, ve, vo, bias, tri)


def _ssm_kernel(u_ref, un_ref, bw_ref, ar_ref, ai_ref, cw_ref, cb_ref, d_ref, wglu_ref, bglu_ref, o_ref,
                st0_ref, st1_ref, ub0_ref, ub1_ref, pin0_ref, pin1_ref, pout_ref, x_ref):
    nb, lt2, width = u_ref.shape
    lt = lt2 // 2
    npair = lt // 2
    prow = nb * npair
    n_set = width // LANES
    set_w = 2 * N_STATES // n_set
    half_set = set_w // 2

    def to_pairs(u, perm_ref):
        uf = u.astype(F32)
        for s in range(n_set):
            for b in range(nb):
                perm_ref[s, b * SSM_PITCH:b * SSM_PITCH + lt, :] = uf[b, :, LANES * s:LANES * (s + 1)]
        step = lambda t, s: perm_ref[s, pl.ds(t, nb, stride=SSM_PITCH), :]
        rows = [jnp.concatenate([step(2 * k + par, s) for s in range(n_set) for par in range(2)], axis=1)
                for k in range(npair)]
        return jnp.concatenate(rows, axis=0).astype(BF16)

    def store_seq_major(y, t0):
        for k in range(npair):
            for par in range(2):
                src = y[par * prow + nb * k:par * prow + nb * (k + 1)]
                for s in range(n_set):
                    pout_ref[s, pl.ds(2 * k + par, nb, stride=SSM_PITCH), :] = src[:, LANES * s:LANES * (s + 1)]
        for b in range(nb):
            o_ref[b, t0:t0 + lt, :] = jnp.concatenate(
                [pout_ref[s, b * SSM_PITCH:b * SSM_PITCH + lt, :] for s in range(n_set)],
                axis=1).astype(BF16)

    def project_in(ub_ref, st_ref):
        for j in range(n_set):
            st_ref[nb:nb + prow, set_w * j:set_w * (j + 1)] = _dot(
                ub_ref[:, 2 * LANES * j:2 * LANES * (j + 1)], bw_ref[j])

    def scan(st_ref):
        st_ref[0:nb, :] = x_ref[...]
        vregs_per_set = half_set // LANES
        for chunk in range(n_set * vregs_per_set // SCAN_SLABS):
            cols = []
            for v in range(chunk * SCAN_SLABS, (chunk + 1) * SCAN_SLABS):
                j, q = divmod(v, vregs_per_set)
                cols.append((set_w * j + LANES * q, set_w * j + half_set + LANES * q, LANES * v))
            ar = [jnp.broadcast_to(ar_ref[:, n:n + LANES], (nb, LANES)) for _, _, n in cols]
            ai = [jnp.broadcast_to(ai_ref[:, n:n + LANES], (nb, LANES)) for _, _, n in cols]
            xr = [st_ref[0:nb, rc:rc + LANES] for rc, _, _ in cols]
            xi = [st_ref[0:nb, ic:ic + LANES] for _, ic, _ in cols]
            for k in range(npair):
                rows = slice(nb * (k + 1), nb * (k + 2))
                for s, (rc, ic, _) in enumerate(cols):
                    r = ar[s] * xr[s] - ai[s] * xi[s] + st_ref[rows, rc:rc + LANES]
                    m = ar[s] * xi[s] + ai[s] * xr[s] + st_ref[rows, ic:ic + LANES]
                    st_ref[rows, rc:rc + LANES] = r
                    st_ref[rows, ic:ic + LANES] = m
                    xr[s], xi[s] = r, m
        x_ref[...] = st_ref[prow:prow + nb, :]

    def project_out(st_ref, ub_ref):
        y2 = [_dot(st_ref[:, set_w * j:set_w * (j + 1)].astype(BF16), cw_ref[j]) for j in range(n_set)]
        y_odd = jnp.concatenate([y[nb:, 0:LANES] for y in y2], axis=1)
        from_state = jnp.concatenate([y[:prow, LANES:2 * LANES] for y in y2], axis=1)
        u_even = jnp.concatenate(
            [ub_ref[:, 2 * LANES * s:2 * LANES * s + LANES] for s in range(n_set)], axis=1)
        u_odd = jnp.concatenate(
            [ub_ref[:, 2 * LANES * s + LANES:2 * LANES * (s + 1)] for s in range(n_set)], axis=1)
        half_w = width // 2
        direct = jnp.concatenate(
            [_dot(u_even[:, half_w * h:half_w * (h + 1)], cb_ref[h]) for h in range(2)], axis=1)
        y = jnp.concatenate([from_state + direct + d_ref[...] * u_even.astype(F32),
                             y_odd + d_ref[...] * u_odd.astype(F32)], axis=0)
        y = y / (1.0 + jnp.exp2(y * (GELU_K1 + GELU_K3 * (y * y))))
        return y * jax.nn.sigmoid(_dot(y.astype(BF16), wglu_ref[...]) + bglu_ref[...])

    @pl.when(pl.program_id(1) == 0)
    def _():
        x_ref[...] = jnp.zeros_like(x_ref)
        ub0_ref[...] = to_pairs(u_ref[:, 0:lt, :], pin0_ref)
        project_in(ub0_ref, st0_ref)

    ub1_ref[...] = to_pairs(u_ref[:, lt:lt2, :], pin1_ref)
    project_in(ub1_ref, st1_ref)
    scan(st0_ref)
    store_seq_major(project_out(st0_ref, ub0_ref), 0)
    ub0_ref[...] = to_pairs(un_ref[...], pin0_ref)
    project_in(ub0_ref, st0_ref)
    scan(st1_ref)
    store_seq_major(project_out(st1_ref, ub1_ref), lt)


def _ssm(u, layer, bw, ar, ai, cw, cb, d, wglu, bglu):
    b, s, _ = u.shape
    assert b % SSM_BATCH == 0 and s % (2 * LT) == 0 and LT % 2 == 0
    last_tile = s // LT - 1
    prow = SSM_BATCH * LT // 2
    tok = pl.BlockSpec((SSM_BATCH, 2 * LT, SSM_WIDTH), lambda i, j: (i, j, 0))
    nxt = pl.BlockSpec((SSM_BATCH, LT, SSM_WIDTH), lambda i, j: (i, jnp.minimum(2 * j + 2, last_tile), 0))
    state_buf = pltpu.VMEM((SSM_BATCH + prow, 2 * N_STATES), F32)
    u_buf = pltpu.VMEM((prow, 2 * SSM_WIDTH), BF16)
    perm_buf = pltpu.VMEM((SSM_WIDTH // LANES, SSM_BATCH * SSM_PITCH, LANES), F32)
    consts = (bw, ar, ai, cw, cb, d, wglu, bglu)
    return pl.pallas_call(
        _ssm_kernel,
        grid=(b // SSM_BATCH, s // (2 * LT)),
        in_specs=[tok, nxt] + [_const_spec(c.shape, layer) for c in consts],
        out_specs=tok,
        out_shape=jax.ShapeDtypeStruct(u.shape, BF16),
        scratch_shapes=[state_buf, state_buf, u_buf, u_buf, perm_buf, perm_buf, perm_buf,
                        pltpu.VMEM((SSM_BATCH, 2 * N_STATES), F32)],
        compiler_params=pltpu.CompilerParams(
            dimension_semantics=("arbitrary", "arbitrary"), vmem_limit_bytes=VMEM_LIMIT_BYTES),
        name="s5",
    )(u, u, *consts)


def _merge_mlp_kernel(x_ref, ya_ref, yb_ref, ga_ref, gb_ref, wa_ref, wb_ref, wo_ref, g_ref,
                      wup_ref, wdn_ref, gfin_ref, o_ref, *, final_norm):
    mixed = (ga_ref[...].astype(F32) * _dot(ya_ref[...], wa_ref[...])
             + gb_ref[...].astype(F32) * _dot(yb_ref[...], wb_ref[...]))
    x = x_ref[...] + _dot(mixed.astype(BF16), wo_ref[...])
    h = _rmsnorm(x, g_ref[...]).astype(BF16)
    for c in range(D_FF // FF_CHUNK):
        cs = slice(c * FF_CHUNK, (c + 1) * FF_CHUNK)
        up = jnp.maximum(_dot(h, wup_ref[:, cs]), 0.0)
        x = x + _dot((up * up).astype(BF16), wdn_ref[cs, :])
    if final_norm:
        x = _rmsnorm(x, gfin_ref[...])
    o_ref[...] = x


def _merge_mlp(x, ya, yb, ga, gb, layer, wa, wb, wo, g, wup, wdn, gfin, final_norm):
    t = x.shape[0]
    tm = min(TM_MLP, t)
    tok = lambda width: pl.BlockSpec((tm, width), lambda i: (i, 0))
    return pl.pallas_call(
        functools.partial(_merge_mlp_kernel, final_norm=final_norm),
        grid=(t // tm,),
        in_specs=[tok(D_MODEL), tok(ATTN_WIDTH), tok(SSM_WIDTH), tok(D_MODEL), tok(D_MODEL),
                  _const_spec(wa.shape, layer), _const_spec(wb.shape, layer),
                  _const_spec(wo.shape, layer), _const_spec(g.shape, layer),
                  _const_spec(wup.shape, layer), _const_spec(wdn.shape, layer),
                  _const_spec(gfin.shape)],
        out_specs=tok(D_MODEL),
        out_shape=jax.ShapeDtypeStruct(x.shape, F32),
        compiler_params=pltpu.CompilerParams(
            dimension_semantics=("arbitrary",), vmem_limit_bytes=VMEM_LIMIT_BYTES),
        name="merge_mlp",
    )(x, ya, yb, ga, gb, wa, wb, wo, g, wup, wdn, gfin)


def _ssm_params(lam_re, lam_im, log_dt, b_re, b_im, c_re, c_im):
    depth = lam_re.shape[0]
    dt = jnp.exp(log_dt)[..., None]
    mag = jnp.exp(lam_re * dt)
    ar = mag * jnp.cos(lam_im * dt)
    ai = mag * jnp.sin(lam_im * dt)
    den = lam_re * lam_re + lam_im * lam_im
    zr = ((ar - 1.0) * lam_re + ai * lam_im) / den
    zi = (ai * lam_re - (ar - 1.0) * lam_im) / den
    bb_re = zr[..., None] * b_re - zi[..., None] * b_im
    bb_im = zr[..., None] * b_im + zi[..., None] * b_re

    a2r = ar * ar - ai * ai
    a2i = 2.0 * ar * ai
    abb_re = ar[..., None] * bb_re - ai[..., None] * bb_im
    abb_im = ar[..., None] * bb_im + ai[..., None] * bb_re
    ca_re = c_re * ar[:, :, None, :] - c_im * ai[:, :, None, :]
    ca_im = c_re * ai[:, :, None, :] + c_im * ar[:, :, None, :]
    cb = (jnp.einsum("lgop,lgpi->lgio", c_re, bb_re) - jnp.einsum("lgop,lgpi->lgio", c_im, bb_im))

    n_set = SSM_WIDTH // LANES
    gs = SSM_GROUPS // n_set
    ch, ns = SSM_GROUP_CH, SSM_STATE

    def block_diag(quad, rows, cols):
        m = jnp.stack([jnp.stack(pair) for pair in quad]).astype(BF16)
        m = m.reshape(2, 2, depth, n_set, gs, cols, rows).transpose(2, 3, 0, 4, 6, 1, 5)
        m = jnp.einsum("lsrgicj,gh->lsrgichj", m, jnp.eye(gs, dtype=BF16))
        return m.reshape(depth, n_set, 2 * gs * rows, 2 * gs * cols)

    bw = block_diag([[abb_re, abb_im], [bb_re, bb_im]], ch, ns)
    cw = block_diag([[c_re, ca_re], [-c_im, -ca_im]], ns, ch)
    hg = SSM_GROUPS // 2
    cbw = jnp.einsum("lsgio,gh->lsgiho", cb.reshape(depth, 2, hg, ch, ch),
                     jnp.eye(hg, dtype=F32)).reshape(depth, 2, hg * ch, hg * ch)
    return (bw.astype(BF16), a2r.reshape(depth, 1, N_STATES), a2i.reshape(depth, 1, N_STATES),
            cw.astype(BF16), cbw.astype(BF16))


def _split_w_in(w_in, b_forget):
    depth = w_in.shape[0]
    o1 = ATTN_WIDTH
    o2 = o1 + ATTN_WIDTH
    o3 = o2 + ATTN_WIDTH
    o4 = o3 + NUM_HEADS
    o5 = o4 + SSM_WIDTH
    o6 = o5 + D_MODEL
    wq = (w_in[:, :, :o1] * (LOG2E * HEAD_DIM ** -0.5)).astype(BF16)
    wf = jnp.zeros((depth, D_MODEL, LANES), F32).at[:, :, :NUM_HEADS].set(w_in[:, :, o3:o4]).astype(BF16)
    bf = jnp.zeros((depth, 1, LANES), F32).at[:, 0, :NUM_HEADS].set(b_forget)
    wb = w_in.astype(BF16)
    return (wq, wb[:, :, o1:o2], wb[:, :, o2:o3], wf, bf,
            wb[:, :, o4:o5], wb[:, :, o5:o6], wb[:, :, o6:])


@jax.jit
def kernel(x, norm_mix, w_in, b_forget, ssm_lambda_re, ssm_lambda_im, ssm_log_dt, ssm_b_re, ssm_b_im,
           ssm_c_re, ssm_c_im, ssm_d, w_glu, b_glu, w_branch_a, w_branch_b, w_out, norm_mlp,
           w_mlp_up, w_mlp_down, norm_final):
    b, s, d = x.shape
    depth = w_in.shape[0]
    t = b * s
    proj = (norm_mix.reshape(depth, 1, d),) + _split_w_in(w_in, b_forget)
    ssm = _ssm_params(ssm_lambda_re, ssm_lambda_im, ssm_log_dt, ssm_b_re, ssm_b_im, ssm_c_re, ssm_c_im)
    ssm = ssm + (ssm_d.reshape(depth, 1, SSM_WIDTH), w_glu.astype(BF16), b_glu.reshape(depth, 1, SSM_WIDTH))
    mlp = (w_branch_a.astype(BF16), w_branch_b.astype(BF16), w_out.astype(BF16),
           norm_mlp.reshape(depth, 1, d), w_mlp_up.astype(BF16), w_mlp_down.astype(BF16))
    gfin = norm_final.reshape(1, d)
    for l in range(depth):
        qe, qo, ke, ko, ve, vo, bias, u, ga, gb = _inproj(x, l, *proj)
        ya = _attention(qe, qo, ke, ko, ve, vo, bias)
        yb = _ssm(u, l, *ssm)
        x = _merge_mlp(x.reshape(t, d), ya.reshape(t, ATTN_WIDTH), yb.reshape(t, SSM_WIDTH),
                       ga.reshape(t, d), gb.reshape(t, d), l, *mlp, gfin,
                       final_norm=(l == depth - 1)).reshape(b, s, d)
    return x
```
